```python
import math
import jax, jax.numpy as jnp
from jax import lax
import numpy as np

D_MODEL = 1024
BATCH = 16
SEQ = 2048
DEPTH = 1

EPS = 1e-6
HG_HEADS = 8
HG_KDIM = 128
HG_VDIM = 128
HG_CHUNK = 64
HG_WIDTH_K = HG_HEADS * HG_KDIM
HG_WIDTH_V = HG_HEADS * HG_VDIM
MLA_HEADS = 8
MLA_Q_RANK = 384
MLA_KV_RANK = 256
MLA_NOPE = 128
MLA_ROPE = 64
MLA_VDIM = 128
MLA_QK_DIM = MLA_NOPE + MLA_ROPE
ROPE_THETA = 10000.0
Q_BLOCK = 128
N_BRANCH = 2
IN_SPLITS = (HG_WIDTH_K, HG_WIDTH_K, HG_WIDTH_V, HG_WIDTH_V, MLA_Q_RANK, MLA_KV_RANK, MLA_ROPE, D_MODEL, D_MODEL)
IN_COLS = sum(IN_SPLITS)
N_GROUPS = 8
EXPERTS_PER_GROUP = 8
N_EXPERTS = N_GROUPS * EXPERTS_PER_GROUP
TOP_K_IN_GROUP = 2
EXPERT_FF = 512
MOE_BLOCK = 128

kernel_name = "hybrid_hgrn2_mla_hmoe_block"


def rmsnorm(x, w):
    xf = x.astype(jnp.float32)
    y = xf * lax.rsqrt(jnp.mean(xf * xf, axis=-1, keepdims=True) + EPS)
    return (y * w.astype(jnp.float32)).astype(x.dtype)


def apply_rope(x, cos, sin):
    x1, x2 = jnp.split(x.astype(jnp.float32), 2, axis=-1)
    return jnp.concatenate([x1 * cos - x2 * sin, x2 * cos + x1 * sin], axis=-1).astype(x.dtype)


def hgrn2_mixer(q_raw, f_raw, i_raw, g_raw, lb, out_norm_w):
    B, S, _ = q_raw.shape
    n = S // HG_CHUNK
    f32 = jnp.float32
    q = jax.nn.silu(q_raw.astype(f32)) * (HG_KDIM ** -0.5)
    lbf = lb.astype(f32)
    fg = lbf + (1.0 - lbf) * jax.nn.sigmoid(f_raw.astype(f32))
    k = 1.0 - fg
    logf = jnp.log(fg)
    v = i_raw.astype(f32)

    def to_chunks(t, d):
        return t.reshape(B, n, HG_CHUNK, HG_HEADS, d).transpose(1, 0, 3, 2, 4)

    qc, kc, lfc = to_chunks(q, HG_KDIM), to_chunks(k, HG_KDIM), to_chunks(logf, HG_KDIM)
    vc = to_chunks(v, HG_VDIM)
    causal = jnp.tril(jnp.ones((HG_CHUNK, HG_CHUNK), dtype=bool))

    def step(state, inp):
        qt, kt, lft, vt = inp
        bcum = jnp.cumsum(lft, axis=2)
        diff = bcum[:, :, :, None, :] - bcum[:, :, None, :, :]
        decay = jnp.exp(jnp.where(causal[None, None, :, :, None], diff, -jnp.inf))
        attn = jnp.sum(qt[:, :, :, None, :] * kt[:, :, None, :, :] * decay, axis=-1)
        o = jnp.einsum('bhts,bhsv->bhtv', attn, vt) + \
            jnp.einsum('bhtk,bhkv->bhtv', qt * jnp.exp(bcum), state)
        b_last = bcum[:, :, -1]
        k_dec = kt * jnp.exp(b_last[:, :, None, :] - bcum)
        state = jnp.exp(b_last)[..., None] * state + jnp.einsum('bhsk,bhsv->bhkv', k_dec, vt)
        return state, o

    s0 = jnp.zeros((B, HG_HEADS, HG_KDIM, HG_VDIM), f32)
    _, o = lax.scan(step, s0, (qc, kc, lfc, vc))
    o = o.transpose(1, 0, 3, 2, 4).reshape(B, S, HG_HEADS, HG_VDIM)
    o = rmsnorm(o, out_norm_w).reshape(B, S, HG_WIDTH_V)
    g = g_raw.astype(f32)
    return o * jax.nn.silu(g)


def mla_mixer(c_q, c_kv, k_rope_raw, positions, q_norm_w, w_uq, kv_norm_w, w_ukv):
    B, S, _ = c_q.shape
    half = MLA_ROPE // 2
    inv_freq = ROPE_THETA ** (-jnp.arange(half, dtype=jnp.float32) / half)
    ang = positions.astype(jnp.float32)[..., None] * inv_freq
    cos, sin = jnp.cos(ang), jnp.sin(ang)

    q = (rmsnorm(c_q, q_norm_w) @ w_uq).reshape(B, S, MLA_HEADS, MLA_QK_DIM)
    q_nope, q_pe = q[..., :MLA_NOPE], q[..., MLA_NOPE:]
    q_pe = apply_rope(q_pe, cos[:, :, None, :], sin[:, :, None, :])
    kv = (rmsnorm(c_kv, kv_norm_w) @ w_ukv).reshape(B, S, MLA_HEADS, MLA_NOPE + MLA_VDIM)
    k_nope, v = kv[..., :MLA_NOPE], kv[..., MLA_NOPE:]
    k_pe = apply_rope(k_rope_raw, cos, sin)
    k = jnp.concatenate([k_nope, jnp.broadcast_to(k_pe[:, :, None, :], (B, S, MLA_HEADS, MLA_ROPE))], axis=-1)
    qf = jnp.concatenate([q_nope, q_pe], axis=-1) * (MLA_QK_DIM ** -0.5)

    nq = S // Q_BLOCK
    qb = qf.reshape(B, nq, Q_BLOCK, MLA_HEADS, MLA_QK_DIM).transpose(1, 0, 3, 2, 4)
    kpos = jnp.arange(S)

    def attend(args):
        qblk, blk = args
        s = jnp.einsum('bhqd,bkhd->bhqk', qblk, k, preferred_element_type=jnp.float32)
        qpos = blk * Q_BLOCK + jnp.arange(Q_BLOCK)
        s = jnp.where(kpos[None, :] <= qpos[:, None], s, -jnp.inf)
        p = jax.nn.softmax(s, axis=-1)
        return jnp.einsum('bhqk,bkhd->bqhd', p.astype(v.dtype), v)

    o = lax.map(attend, (qb, jnp.arange(nq)))
    return o.transpose(1, 0, 2, 3, 4).reshape(B, S, MLA_HEADS * MLA_VDIM)


def hierarchical_moe(h, w_group, b_group, w_expert, b_expert, w1, w3, w2):
    B, S, D = h.shape
    T = B * S
    xt = h.reshape(T, D)
    g_prob = jax.nn.softmax((xt @ w_group).astype(jnp.float32) + b_group.astype(jnp.float32), axis=-1)
    g_w, g_idx = lax.top_k(g_prob, 1)
    e_logits = ((xt @ w_expert).astype(jnp.float32) + b_expert.astype(jnp.float32)).reshape(T, N_GROUPS, EXPERTS_PER_GROUP)
    e_sel = jnp.take_along_axis(e_logits, g_idx[:, :, None], axis=1)[:, 0]
    e_prob = jax.nn.softmax(e_sel, axis=-1)
    e_w, e_idx = lax.top_k(e_prob, TOP_K_IN_GROUP)
    e_w = e_w / jnp.sum(e_w, axis=-1, keepdims=True)
    weights = g_w * e_w
    expert_ids = g_idx * EXPERTS_PER_GROUP + e_idx

    A = T * TOP_K_IN_GROUP
    flat_e = expert_ids.reshape(A)
    flat_t = jnp.repeat(jnp.arange(T, dtype=jnp.int32), TOP_K_IN_GROUP)
    flat_w = weights.reshape(A)
    order = jnp.argsort(flat_e)
    se, st, sw = flat_e[order], flat_t[order], flat_w[order]
    counts = jax.ops.segment_sum(jnp.ones_like(se), se, num_segments=N_EXPERTS)
    starts = jnp.cumsum(counts) - counts
    padded = (counts + MOE_BLOCK - 1) // MOE_BLOCK * MOE_BLOCK
    pends = jnp.cumsum(padded)
    pstarts = pends - padded
    dest = pstarts[se] + (jnp.arange(A, dtype=se.dtype) - starts[se])
    n_blocks = A // MOE_BLOCK + N_EXPERTS
    cap = n_blocks * MOE_BLOCK
    tok_buf = jnp.full((cap,), T, dtype=jnp.int32).at[dest].set(st)
    w_buf = jnp.zeros((cap,), jnp.float32).at[dest].set(sw)
    block_start = jnp.arange(n_blocks, dtype=pends.dtype) * MOE_BLOCK
    block_expert = jnp.minimum(jnp.searchsorted(pends, block_start, side='right'), N_EXPERTS - 1)
    x_pad = jnp.concatenate([xt, jnp.zeros((1, D), xt.dtype)], axis=0)

    def run_block(args):
        toks, wts, e = args
        xb = x_pad[toks]
        hid = jax.nn.silu(xb @ w1[e]) * (xb @ w3[e])
        return (hid @ w2[e]) * wts[:, None].astype(xb.dtype)

    yb = lax.map(run_block, (tok_buf.reshape(n_blocks, MOE_BLOCK), w_buf.reshape(n_blocks, MOE_BLOCK), block_expert))
    out = jax.ops.segment_sum(yb.reshape(cap, D), tok_buf, num_segments=T + 1)[:T]
    return out.reshape(B, S, D)


def setup_inputs(seed: int = 0) -> dict:
    key = jax.random.key(seed)
    ks = jax.random.split(key, 24)
    f32 = jnp.float32
    nrm = lambda k, shape, scale: jax.random.normal(k, shape, f32) * scale
    gain = lambda k, shape: 1.0 + 0.02 * jax.random.normal(k, shape, f32)
    x = jax.random.normal(ks[0], (BATCH, SEQ, D_MODEL), f32)
    offsets = jax.random.randint(ks[1], (BATCH, 1), 0, 4096, dtype=jnp.int32)
    positions = offsets + jnp.arange(SEQ, dtype=jnp.int32)[None, :]
    return {
        "x": x,
        "positions": positions,
        "attn_norm_w": gain(ks[2], (DEPTH, D_MODEL)),
        "w_in": nrm(ks[3], (DEPTH, D_MODEL, IN_COLS), D_MODEL ** -0.5),
        "hg_lower_bound": nrm(ks[4], (DEPTH + 1, HG_WIDTH_K), 0.5),
        "hg_out_norm_w": gain(ks[5], (DEPTH, HG_VDIM)),
        "mla_q_norm_w": gain(ks[6], (DEPTH, MLA_Q_RANK)),
        "mla_w_uq": nrm(ks[7], (DEPTH, MLA_Q_RANK, MLA_HEADS * MLA_QK_DIM), MLA_Q_RANK ** -0.5),
        "mla_kv_norm_w": gain(ks[8], (DEPTH, MLA_KV_RANK)),
        "mla_w_ukv": nrm(ks[9], (DEPTH, MLA_KV_RANK, MLA_HEADS * (MLA_NOPE + MLA_VDIM)), MLA_KV_RANK ** -0.5),
        "w_branch_hgrn": nrm(ks[10], (DEPTH, HG_WIDTH_V, D_MODEL), HG_WIDTH_V ** -0.5),
        "w_branch_mla": nrm(ks[11], (DEPTH, MLA_HEADS * MLA_VDIM, D_MODEL), (MLA_HEADS * MLA_VDIM) ** -0.5),
        "w_out": nrm(ks[12], (DEPTH, D_MODEL, D_MODEL), D_MODEL ** -0.5),
        "ffn_norm_w": gain(ks[13], (DEPTH, D_MODEL)),
        "router_group_w": nrm(ks[14], (DEPTH, D_MODEL, N_GROUPS), D_MODEL ** -0.5),
        "router_group_b": nrm(ks[15], (DEPTH, N_GROUPS), 0.01),
        "router_expert_w": nrm(ks[16], (DEPTH, D_MODEL, N_EXPERTS), D_MODEL ** -0.5),
        "router_expert_b": nrm(ks[17], (DEPTH, N_EXPERTS), 0.01),
        "expert_w1": nrm(ks[18], (DEPTH, N_EXPERTS, D_MODEL, EXPERT_FF), D_MODEL ** -0.5),
        "expert_w3": nrm(ks[19], (DEPTH, N_EXPERTS, D_MODEL, EXPERT_FF), D_MODEL ** -0.5),
        "expert_w2": nrm(ks[20], (DEPTH, N_EXPERTS, EXPERT_FF, D_MODEL), EXPERT_FF ** -0.5),
        "final_norm_w": gain(ks[21], (D_MODEL,)),
    }


def reference(x, positions, attn_norm_w, w_in, hg_lower_bound, hg_out_norm_w, mla_q_norm_w, mla_w_uq,
              mla_kv_norm_w, mla_w_ukv, w_branch_hgrn, w_branch_mla, w_out, ffn_norm_w, router_group_w,
              router_group_b, router_expert_w, router_expert_b, expert_w1, expert_w3, expert_w2, final_norm_w):
    split_idx = [int(v) for v in np.cumsum(IN_SPLITS)[:-1]]
    lb_all = jnp.cumsum(jax.nn.softmax(hg_lower_bound.astype(jnp.float32), axis=0), axis=0)
    for l in range(DEPTH):
        h = rmsnorm(x, attn_norm_w[l])
        z = h @ w_in[l]
        hq, hf, hi, hg, cq, ckv, kr, gate_a, gate_b = jnp.split(z, split_idx, axis=-1)
        o_a = hgrn2_mixer(hq, hf, hi, hg, lb_all[l], hg_out_norm_w[l]).astype(x.dtype)
        o_b = mla_mixer(cq, ckv, kr, positions, mla_q_norm_w[l], mla_w_uq[l], mla_kv_norm_w[l], mla_w_ukv[l])
        y = jax.nn.sigmoid(gate_a) * (o_a @ w_branch_hgrn[l]) + jax.nn.sigmoid(gate_b) * (o_b @ w_branch_mla[l])
        x = x + y @ w_out[l]
        h = rmsnorm(x, ffn_norm_w[l])
        x = x + hierarchical_moe(h, router_group_w[l], router_group_b[l], router_expert_w[l], router_expert_b[l],
                                 expert_w1[l], expert_w3[l], expert_w2[l])
    return rmsnorm(x, final_norm_w)
```

```python
import functools
import math

import numpy as np
import jax
import jax.numpy as jnp
from jax import lax
from jax.experimental import pallas as pl
from jax.experimental.pallas import tpu as pltpu

F32 = jnp.float32
BF16 = jnp.bfloat16
I32 = jnp.int32

EPS = 1e-6
LANES = 128
HG_HEADS = 8
HG_DIM = 128
MLA_HEADS = 8
MLA_Q_RANK = 384
MLA_KV_RANK = 256
MLA_NOPE = 128
MLA_ROPE = 64
MLA_VDIM = 128
MLA_QK_DIM = MLA_NOPE + MLA_ROPE
MLA_QK_PAD = 256
ROPE_THETA = 10000.0
N_GROUPS = 8
EXPERTS_PER_GROUP = 8
N_EXPERTS = N_GROUPS * EXPERTS_PER_GROUP
EXPERT_LANE0 = N_GROUPS
EXPERT_FF = 512

HG_CHUNK = 128
HG_LEVELS = 7
HG_HEADS_PER_STEP = 2
MOE_ROWS = 256
VMEM_LIMIT = 56 * 1024 * 1024


def _cparams(sem):
    return pltpu.CompilerParams(dimension_semantics=sem, vmem_limit_bytes=VMEM_LIMIT)


def _dot(a, b):
    return jnp.dot(a, b, preferred_element_type=F32)


def _dot_nt(a, b):
    return lax.dot_general(a, b, (((1,), (1,)), ((), ())), preferred_element_type=F32)


def _dot_tn(a, b):
    return lax.dot_general(a, b, (((0,), (0,)), ((), ())), preferred_element_type=F32)


def _split_bf16(x):
    hi = x.astype(BF16)
    lo = (x - hi.astype(F32)).astype(BF16)
    return hi, lo


_IN_OFF_Q = 0
_IN_OFF_F = 1024
_IN_OFF_I = 2048
_IN_OFF_G = 3072
_IN_OFF_CQ = 4096
_IN_OFF_CKV = _IN_OFF_CQ + MLA_Q_RANK
_IN_OFF_KR = _IN_OFF_CKV + MLA_KV_RANK
_IN_OFF_GA = _IN_OFF_KR + LANES
_IN_OFF_GB = _IN_OFF_GA + 1024
_IN_COLS_PAD = _IN_OFF_GB + 1024


def _rope_lane_layout(w):
    half = MLA_ROPE // 2
    z = jnp.zeros((w.shape[0], half), w.dtype)
    return jnp.concatenate([w[:, :half], z, w[:, half:], z], axis=1)


def _inproj_kernel(x_ref, nw_ref, lbt_ref, w_ref, q_ref, kk_ref, lf_ref, v_ref, g_ref, cq_ref, ckv_ref,
                   kr_ref, sa_ref, sb_ref, h_scr, *, layer):
    x = x_ref[...]
    ms = jnp.mean(x * x, axis=-1, keepdims=True)
    h_scr[...] = (x * lax.rsqrt(ms + EPS) * nw_ref[...]).astype(BF16)

    tab = lbt_ref[...]
    e = jnp.exp(tab - jnp.max(tab, axis=0, keepdims=True))
    sm = e / jnp.sum(e, axis=0, keepdims=True)
    lb = jnp.sum(sm[:layer + 1], axis=0, keepdims=True)

    def proj(off, width):
        return _dot(h_scr[...], w_ref[:, off:off + width])

    half = 512
    for c in range(0, 1024, half):
        z = proj(_IN_OFF_Q + c, half)
        q_ref[:, c:c + half] = (z * jax.nn.sigmoid(z) * (HG_DIM ** -0.5)).astype(BF16)
        z = proj(_IN_OFF_F + c, half)
        lbc = lb[:, c:c + half]
        fg = lbc + (1.0 - lbc) * jax.nn.sigmoid(z)
        lf_ref[:, c:c + half] = jnp.log(fg)
        kk_ref[:, c:c + half] = (1.0 - fg).astype(BF16)
        v_ref[:, c:c + half] = proj(_IN_OFF_I + c, half).astype(BF16)
        z = proj(_IN_OFF_G + c, half)
        g_ref[:, c:c + half] = (z * jax.nn.sigmoid(z)).astype(BF16)
        sa_ref[:, c:c + half] = jax.nn.sigmoid(proj(_IN_OFF_GA + c, half)).astype(BF16)
        sb_ref[:, c:c + half] = jax.nn.sigmoid(proj(_IN_OFF_GB + c, half)).astype(BF16)
    cq_ref[...] = proj(_IN_OFF_CQ, MLA_Q_RANK)
    ckv_ref[...] = proj(_IN_OFF_CKV, MLA_KV_RANK)
    kr_ref[...] = proj(_IN_OFF_KR, LANES)


def _in_proj(x2, attn_norm_w, w_in, lb_table, layer, tm):
    t, d = x2.shape
    parts = []
    off = 0
    for width in (1024, 1024, 1024, 1024, MLA_Q_RANK, MLA_KV_RANK, MLA_ROPE, 1024, 1024):
        parts.append(w_in[:, off:off + width])
        off += width
    parts[6] = _rope_lane_layout(parts[6])
    w = jnp.concatenate(parts, axis=1).astype(BF16)
    assert w.shape[1] == _IN_COLS_PAD
    row = lambda width: pl.BlockSpec((tm, width), lambda i: (i, 0))
    full = lambda a: pl.BlockSpec(a.shape, lambda i: (0,) * a.ndim)
    nw = attn_norm_w.reshape(1, d)
    out_shape = [jax.ShapeDtypeStruct((t, 1024), BF16),
                 jax.ShapeDtypeStruct((t, 1024), BF16),
                 jax.ShapeDtypeStruct((t, 1024), F32),
                 jax.ShapeDtypeStruct((t, 1024), BF16),
                 jax.ShapeDtypeStruct((t, 1024), BF16),
                 jax.ShapeDtypeStruct((t, MLA_Q_RANK), F32),
                 jax.ShapeDtypeStruct((t, MLA_KV_RANK), F32),
                 jax.ShapeDtypeStruct((t, LANES), F32),
                 jax.ShapeDtypeStruct((t, 1024), BF16),
                 jax.ShapeDtypeStruct((t, 1024), BF16)]
    return pl.pallas_call(
        functools.partial(_inproj_kernel, layer=layer),
        grid=(t // tm,),
        in_specs=[row(d), full(nw), full(lb_table),
                  pl.BlockSpec(w.shape, lambda i: (0, 0), pipeline_mode=pl.Buffered(1))],
        out_specs=[row(s.shape[1]) for s in out_shape],
        out_shape=out_shape,
        scratch_shapes=[pltpu.VMEM((tm, d), BF16)],
        compiler_params=_cparams(("parallel",)),
        name="in_proj",
    )(x2, nw, lb_table, w)


def _hgrn_tables():
    n = HG_CHUNK
    mats = []
    r = np.arange(n)[:, None]
    j = np.arange(n)[None, :]
    for lvl in range(HG_LEVELS):
        m = n >> (lvl + 1)
        c = (r // (2 * m)) * (2 * m) + m - 1
        odd = (r & m) != 0
        mats.append(np.where(odd, (j > c) & (j <= r), (j > r) & (j <= c)))
    mats.append(j <= r)
    return jnp.asarray(np.concatenate(mats, axis=0).astype(np.float32), dtype=BF16)


def _hgrn_kernel(q_ref, kk_ref, lf_ref, v_ref, g_ref, nw_ref, tab_ref, o_ref, st_ref, *, heads, n_chunks):
    n = HG_CHUNK
    row = lax.broadcasted_iota(I32, (n, n), 0)
    col = lax.broadcasted_iota(I32, (n, n), 1)
    xr = row ^ col
    st_ref[...] = jnp.zeros_like(st_ref)
    nw = nw_ref[...]

    def chunk(c, carry):
        r0 = pl.multiple_of(c * n, n)
        for h in range(heads):
            ls = slice(h * HG_DIM, (h + 1) * HG_DIM)
            lf = lf_ref[pl.ds(r0, n), ls]
            qb = q_ref[pl.ds(r0, n), ls]
            kb = kk_ref[pl.ds(r0, n), ls]
            vb = v_ref[pl.ds(r0, n), ls]
            q = qb.astype(F32)
            k = kb.astype(F32)
            hi, lo = _split_bf16(lf)
            ex = _dot(tab_ref[...], jnp.concatenate([hi, lo], axis=1))
            ex = ex[:, :HG_DIM] + ex[:, HG_DIM:]
            b = ex[HG_LEVELS * n:, :]

            a = jnp.where(xr == 0, _dot_nt(qb, kb), 0.0)
            for lvl in range(HG_LEVELS):
                m = n >> (lvl + 1)
                odd = (row & m) != 0
                p = jnp.where(odd, q, k) * jnp.exp(ex[lvl * n:(lvl + 1) * n, :])
                qp = jnp.where(odd, p, 0.0).astype(BF16)
                kp = jnp.where(odd, 0.0, p).astype(BF16)
                a = jnp.where((xr >= m) & (xr < 2 * m), _dot_nt(qp, kp), a)

            st = st_ref[h]
            b_last = b[n - 1:n, :]
            o = _dot(a.astype(BF16), vb) + _dot_nt((q * jnp.exp(b)).astype(BF16), st.astype(BF16))
            kdec = (k * jnp.exp(b_last - b)).astype(BF16)
            st_ref[h] = st * jnp.exp(b_last) + _dot_tn(vb, kdec)

            ms = jnp.mean(o * o, axis=-1, keepdims=True)
            y = o * lax.rsqrt(ms + EPS) * nw
            o_ref[pl.ds(r0, n), ls] = (y * g_ref[pl.ds(r0, n), ls].astype(F32)).astype(BF16)
        return carry

    lax.fori_loop(0, n_chunks, chunk, 0)


def _hgrn(q, kk, lf, v, g, out_norm_w, batch, seq):
    t, width = q.shape
    hp = HG_HEADS_PER_STEP
    wblk = hp * HG_DIM
    tab = _hgrn_tables()
    spec = pl.BlockSpec((seq, wblk), lambda b, j: (b, j))
    full = lambda a: pl.BlockSpec(a.shape, lambda b, j: (0,) * a.ndim)
    nw = out_norm_w.reshape(1, HG_DIM)
    return pl.pallas_call(
        functools.partial(_hgrn_kernel, heads=hp, n_chunks=seq // HG_CHUNK),
        grid=(batch, width // wblk),
        in_specs=[spec, spec, spec, spec, spec, full(nw), full(tab)],
        out_specs=spec,
        out_shape=jax.ShapeDtypeStruct((t, width), BF16),
        scratch_shapes=[pltpu.VMEM((hp, HG_DIM, HG_DIM), F32)],
        compiler_params=_cparams(("parallel", "parallel")),
        name="hgrn",
    )(q, kk, lf, v, g, nw, tab)


def _rope(p, cos_t, sin_t):
    return p * cos_t + pltpu.roll(p, LANES // 2, axis=1) * sin_t


def _mla_prep_kernel(cq_ref, ckv_ref, kr_ref, pos_ref, qnw_ref, kvnw_ref, wq_ref, wkv_ref, rot_ref,
                     q_ref, k_ref, v_ref):
    ang = pos_ref[...] * rot_ref[0:1, :]
    cos_t = jnp.cos(ang) * rot_ref[1:2, :]
    sin_t = jnp.sin(ang) * rot_ref[2:3, :]

    cq = cq_ref[...]
    cqn = cq * lax.rsqrt(jnp.mean(cq * cq, axis=-1, keepdims=True) + EPS) * qnw_ref[...]
    ckv = ckv_ref[...]
    ckvn = ckv * lax.rsqrt(jnp.mean(ckv * ckv, axis=-1, keepdims=True) + EPS) * kvnw_ref[...]
    cqn = cqn.astype(BF16)
    ckvn = ckvn.astype(BF16)
    k_pe = _rope(kr_ref[...], cos_t, sin_t).astype(BF16)
    scale = MLA_QK_DIM ** -0.5
    for h in range(MLA_HEADS):
        c0 = h * MLA_QK_PAD
        qh = _dot(cqn, wq_ref[:, c0:c0 + MLA_QK_PAD])
        q_ref[h, :, 0:MLA_NOPE] = (qh[:, :MLA_NOPE] * scale).astype(BF16)
        q_ref[h, :, MLA_NOPE:MLA_QK_PAD] = (_rope(qh[:, MLA_NOPE:], cos_t, sin_t) * scale).astype(BF16)
        kvh = _dot(ckvn, wkv_ref[:, c0:c0 + MLA_NOPE + MLA_VDIM])
        k_ref[h, :, 0:MLA_NOPE] = kvh[:, :MLA_NOPE].astype(BF16)
        k_ref[h, :, MLA_NOPE:MLA_QK_PAD] = k_pe
        v_ref[h] = kvh[:, MLA_NOPE:].astype(BF16)


def _mla_prep(cq, ckv, kr, positions, q_norm_w, w_uq, kv_norm_w, w_ukv, batch, seq, tm):
    t = cq.shape[0]
    half = MLA_ROPE // 2
    wq = w_uq.reshape(MLA_Q_RANK, MLA_HEADS, MLA_QK_DIM)
    zq = jnp.zeros((MLA_Q_RANK, MLA_HEADS, half), w_uq.dtype)
    wq = jnp.concatenate([wq[:, :, :MLA_NOPE], wq[:, :, MLA_NOPE:MLA_NOPE + half], zq,
                          wq[:, :, MLA_NOPE + half:], zq], axis=2)
    wq = wq.reshape(MLA_Q_RANK, MLA_HEADS * MLA_QK_PAD).astype(BF16)
    wkv = w_ukv.astype(BF16)
    inv_freq = ROPE_THETA ** (-jnp.arange(half, dtype=F32) / half)
    z = jnp.zeros((half,), F32)
    o = jnp.ones((half,), F32)
    rot = jnp.stack([jnp.concatenate([inv_freq, z, inv_freq, z]),
                     jnp.concatenate([o, z, o, z]),
                     jnp.concatenate([-o, z, o, z])])
    pos = positions.astype(F32).reshape(t, 1)
    nblk = seq // tm
    row = lambda width: pl.BlockSpec((tm, width), lambda i: (i, 0))
    full = lambda a: pl.BlockSpec(a.shape, lambda i: (0,) * a.ndim)
    hspec = lambda width: pl.BlockSpec((None, MLA_HEADS, tm, width), lambda i: (i // nblk, 0, i % nblk, 0))
    qnw = q_norm_w.reshape(1, -1)
    kvnw = kv_norm_w.reshape(1, -1)
    return pl.pallas_call(
        _mla_prep_kernel,
        grid=(t // tm,),
        in_specs=[row(MLA_Q_RANK), row(MLA_KV_RANK), row(LANES), row(1), full(qnw), full(kvnw),
                  full(wq), full(wkv), full(rot)],
        out_specs=[hspec(MLA_QK_PAD), hspec(MLA_QK_PAD), hspec(MLA_VDIM)],
        out_shape=[jax.ShapeDtypeStruct((batch, MLA_HEADS, seq, MLA_QK_PAD), BF16),
                   jax.ShapeDtypeStruct((batch, MLA_HEADS, seq, MLA_QK_PAD), BF16),
                   jax.ShapeDtypeStruct((batch, MLA_HEADS, seq, MLA_VDIM), BF16)],
        compiler_params=_cparams(("parallel",)),
        name="mla_prep",
    )(cq, ckv, kr, pos, qnw, kvnw, wq, wkv, rot)


def _flash_kernel(q_ref, k_ref, v_ref, o_ref, *, tq):
    seq = q_ref.shape[0]
    row = lax.broadcasted_iota(I32, (tq, tq), 0)
    col = lax.broadcasted_iota(I32, (tq, tq), 1)
    causal = row >= col
    for i in range(seq // tq):
        lo, hi = i * tq, (i + 1) * tq
        q = q_ref[lo:hi, :]
        sd = jnp.where(causal, _dot_nt(q, k_ref[lo:hi, :]), -jnp.inf)
        m = jnp.max(sd, axis=-1, keepdims=True)
        if i > 0:
            so = _dot_nt(q, k_ref[0:lo, :])
            m = jnp.maximum(m, jnp.max(so, axis=-1, keepdims=True))
            po = jnp.exp(so - m)
            den = jnp.sum(po, axis=-1, keepdims=True)
            acc = _dot(po.astype(BF16), v_ref[0:lo, :])
        pd = jnp.exp(sd - m)
        if i > 0:
            den = den + jnp.sum(pd, axis=-1, keepdims=True)
            acc = acc + _dot(pd.astype(BF16), v_ref[lo:hi, :])
        else:
            den = jnp.sum(pd, axis=-1, keepdims=True)
            acc = _dot(pd.astype(BF16), v_ref[lo:hi, :])
        o_ref[lo:hi, :] = (acc / den).astype(BF16)


def _flash(q, k, v, tq):
    batch, heads, seq, _ = q.shape
    qk = pl.BlockSpec((None, None, seq, MLA_QK_PAD), lambda b, h: (b, h, 0, 0))
    return pl.pallas_call(
        functools.partial(_flash_kernel, tq=tq),
        grid=(batch, heads),
        in_specs=[qk, qk, pl.BlockSpec((None, None, seq, MLA_VDIM), lambda b, h: (b, h, 0, 0))],
        out_specs=pl.BlockSpec((None, seq, MLA_VDIM), lambda b, h: (b, 0, h)),
        out_shape=jax.ShapeDtypeStruct((batch, seq, heads * MLA_VDIM), BF16),
        compiler_params=_cparams(("parallel", "parallel")),
        name="flash",
    )(q, k, v)


_R_E0, _R_E1, _R_W0, _R_W1, _R_RANK0, _R_RANK1 = 0, 1, 2, 3, 4, 5


def _first_lane_equal(x, value, lane):
    return jnp.min(jnp.where(x == value, lane, LANES), axis=-1, keepdims=True)


def _combine_kernel(oa_ref, ob_ref, sa_ref, sb_ref, x_ref, wa_ref, wb_ref, wo_ref, fnw_ref, wr_ref, br_ref,
                    lt_ref, x1_ref, h2_ref, rec_ref, cnt_ref, run_ref):
    i = pl.program_id(0)

    @pl.when(i == 0)
    def _():
        run_ref[...] = jnp.zeros_like(run_ref)

    ya = _dot(oa_ref[...], wa_ref[...])
    yb = _dot(ob_ref[...], wb_ref[...])
    y = sa_ref[...].astype(F32) * ya + sb_ref[...].astype(F32) * yb
    x1 = x_ref[...] + _dot(y.astype(BF16), wo_ref[...])
    x1_ref[...] = x1
    h2 = x1 * lax.rsqrt(jnp.mean(x1 * x1, axis=-1, keepdims=True) + EPS) * fnw_ref[...]
    h2_ref[...] = h2

    hh, hl = _split_bf16(h2)
    logits = _dot(hh, wr_ref[0]) + _dot(hl, wr_ref[0]) + _dot(hh, wr_ref[1]) + br_ref[...]
    tm = logits.shape[0]
    lane = lax.broadcasted_iota(I32, (tm, LANES), 1)
    neg = -jnp.inf

    gl = jnp.where(lane < N_GROUPS, logits, neg)
    ge = jnp.exp(gl - jnp.max(gl, axis=-1, keepdims=True))
    gp = ge / jnp.sum(ge, axis=-1, keepdims=True)
    g_w = jnp.max(gp, axis=-1, keepdims=True)
    g_idx = _first_lane_equal(gp, g_w, lane)

    ej = lane - EXPERT_LANE0
    sel = (ej >= 0) & (ej < N_EXPERTS) & ((ej >> 3) == g_idx)
    el = jnp.where(sel, logits, neg)
    ee = jnp.exp(el - jnp.max(el, axis=-1, keepdims=True))
    ep = jnp.where(sel, ee / jnp.sum(ee, axis=-1, keepdims=True), -1.0)
    p0 = jnp.max(ep, axis=-1, keepdims=True)
    l0 = _first_lane_equal(ep, p0, lane)
    ep1 = jnp.where(lane == l0, -1.0, ep)
    p1 = jnp.max(ep1, axis=-1, keepdims=True)
    l1 = _first_lane_equal(ep1, p1, lane)
    psum = p0 + p1
    w0 = g_w * (p0 / psum)
    w1 = g_w * (p1 / psum)

    oh0 = lane == l0
    oh1 = lane == l1
    oh = jnp.where(oh0 | oh1, 1.0, 0.0)
    before = run_ref[...] + _dot(lt_ref[...], oh.astype(BF16))
    rank0 = jnp.sum(jnp.where(oh0, before, 0.0), axis=-1, keepdims=True)
    rank1 = jnp.sum(jnp.where(oh1, before, 0.0), axis=-1, keepdims=True)
    run = run_ref[...] + jnp.sum(oh, axis=0, keepdims=True)
    run_ref[...] = run
    cnt_ref[...] = run

    rec = jnp.where(lane == _R_E0, (l0 - EXPERT_LANE0).astype(F32), 0.0)
    rec = jnp.where(lane == _R_E1, (l1 - EXPERT_LANE0).astype(F32), rec)
    rec = jnp.where(lane == _R_W0, w0, rec)
    rec = jnp.where(lane == _R_W1, w1, rec)
    rec = jnp.where(lane == _R_RANK0, rank0, rec)
    rec = jnp.where(lane == _R_RANK1, rank1, rec)
    rec_ref[...] = rec


def _combine(o_a, o_b, sa, sb, x2, w_a, w_b, w_o, ffn_norm_w, w_group, b_group, w_expert, b_expert, tm):
    t, d = x2.shape
    pad = LANES - N_GROUPS - N_EXPERTS
    wr = jnp.concatenate([w_group, w_expert, jnp.zeros((d, pad), F32)], axis=1)
    wr_hi = wr.astype(BF16)
    wr_lo = (wr - wr_hi.astype(F32)).astype(BF16)
    wr2 = jnp.stack([wr_hi, wr_lo])
    br = jnp.concatenate([b_group, b_expert, jnp.zeros((pad,), F32)]).reshape(1, LANES)
    lt = jnp.asarray(np.tril(np.ones((tm, tm), np.float32), -1), dtype=BF16)
    fnw = ffn_norm_w.reshape(1, d)
    row = lambda width: pl.BlockSpec((tm, width), lambda i: (i, 0))
    full = lambda a: pl.BlockSpec(a.shape, lambda i: (0,) * a.ndim)
    wa, wb, wo = w_a.astype(BF16), w_b.astype(BF16), w_o.astype(BF16)
    return pl.pallas_call(
        _combine_kernel,
        grid=(t // tm,),
        in_specs=[row(d), row(d), row(d), row(d), row(d), full(wa), full(wb), full(wo), full(fnw),
                  full(wr2), full(br), full(lt)],
        out_specs=[row(d), row(d), row(LANES), pl.BlockSpec((1, LANES), lambda i: (0, 0))],
        out_shape=[jax.ShapeDtypeStruct((t, d), F32),
                   jax.ShapeDtypeStruct((t, d), F32),
                   jax.ShapeDtypeStruct((t, LANES), F32),
                   jax.ShapeDtypeStruct((1, LANES), F32)],
        scratch_shapes=[pltpu.VMEM((1, LANES), F32)],
        compiler_params=_cparams(("arbitrary",)),
        name="combine",
    )(o_a, o_b, sa, sb, x2, wa, wb, wo, fnw, wr2, br, lt)


def _row_copy(src_ref, src_row, dst_ref, dst_row, sem):
    return pltpu.make_async_copy(src_ref.at[pl.ds(src_row, 1)], dst_ref.at[pl.ds(dst_row, 1)], sem)


def _dispatch_kernel(pstart_ref, e_ref, r_ref, h_ref, xs_in_ref, xs_ref, sem):
    del xs_in_ref
    tm = h_ref.shape[0]

    def start(j, carry):
        for k in range(2):
            a = 2 * j + k
            _row_copy(h_ref, j, xs_ref, pstart_ref[e_ref[a]] + r_ref[a], sem).start()
        return carry

    lax.fori_loop(0, tm, start, 0)

    def wait(j, carry):
        for k in range(2):
            _row_copy(h_ref, 0, xs_ref, 0, sem).wait()
        return carry

    lax.fori_loop(0, tm, wait, 0)


def _dispatch(h2, e01, r01, pstart, cap, tm):
    t, d = h2.shape
    xs0 = jnp.zeros((cap, d), h2.dtype)
    smem = pl.BlockSpec((2 * tm,), lambda i, ps: (i,), memory_space=pltpu.SMEM)
    return pl.pallas_call(
        _dispatch_kernel,
        grid_spec=pltpu.PrefetchScalarGridSpec(
            num_scalar_prefetch=1, grid=(t // tm,),
            in_specs=[smem, smem, pl.BlockSpec((tm, d), lambda i, ps: (i, 0)),
                      pl.BlockSpec(memory_space=pl.ANY)],
            out_specs=pl.BlockSpec(memory_space=pl.ANY),
            scratch_shapes=[pltpu.SemaphoreType.DMA(())]),
        out_shape=jax.ShapeDtypeStruct((cap, d), h2.dtype),
        input_output_aliases={4: 0},
        compiler_params=_cparams(("arbitrary",)),
        name="dispatch",
    )(pstart, e01, r01, h2, xs0)


def _experts_kernel(be_ref, nu_ref, xs_ref, w1_ref, w3_ref, w2_ref, ys_ref, w1b, w3b, w2b):
    i = pl.program_id(0)

    @pl.when(i < nu_ref[0])
    def _():
        changed = jnp.logical_or(i == 0, be_ref[i] != be_ref[jnp.maximum(i - 1, 0)])

        @pl.when(changed)
        def _():
            w1b[...] = w1_ref[...].astype(BF16)
            w3b[...] = w3_ref[...].astype(BF16)
            w2b[...] = w2_ref[...].astype(BF16)

        x = xs_ref[...].astype(BF16)
        h1 = _dot(x, w1b[...])
        h3 = _dot(x, w3b[...])
        hid = h1 * jax.nn.sigmoid(h1) * h3
        ys_ref[...] = _dot(hid.astype(BF16), w2b[...])

    @pl.when(i >= nu_ref[0])
    def _():
        ys_ref[...] = jnp.zeros_like(ys_ref)


def _experts(xs, w1, w3, w2, block_expert, n_used):
    cap, d = xs.shape
    bm = MOE_ROWS
    ff = w1.shape[2]
    rows = pl.BlockSpec((bm, d), lambda i, be, nu: (jnp.minimum(i, nu[0] - 1), 0))
    return pl.pallas_call(
        _experts_kernel,
        grid_spec=pltpu.PrefetchScalarGridSpec(
            num_scalar_prefetch=2, grid=(cap // bm,),
            in_specs=[rows,
                      pl.BlockSpec((None, d, ff), lambda i, be, nu: (be[i], 0, 0)),
                      pl.BlockSpec((None, d, ff), lambda i, be, nu: (be[i], 0, 0)),
                      pl.BlockSpec((None, ff, d), lambda i, be, nu: (be[i], 0, 0))],
            out_specs=pl.BlockSpec((bm, d), lambda i, be, nu: (i, 0)),
            scratch_shapes=[pltpu.VMEM((d, ff), BF16), pltpu.VMEM((d, ff), BF16), pltpu.VMEM((ff, d), BF16)]),
        out_shape=jax.ShapeDtypeStruct((cap, d), F32),
        compiler_params=_cparams(("arbitrary",)),
        name="experts",
    )(block_expert, n_used, xs, w1, w3, w2)


def _final_kernel(pstart_ref, e_ref, r_ref, x1_ref, rec_ref, nw_ref, ys_ref, o_ref, buf, sem):
    tm = x1_ref.shape[0]

    def start(j, carry):
        for k in range(2):
            a = 2 * j + k
            _row_copy(ys_ref, pstart_ref[e_ref[a]] + r_ref[a], buf.at[k], j, sem).start()
        return carry

    lax.fori_loop(0, tm, start, 0)

    def wait(j, carry):
        for k in range(2):
            _row_copy(ys_ref, 0, buf.at[k], 0, sem).wait()
        return carry

    lax.fori_loop(0, tm, wait, 0)

    rec = rec_ref[...]
    x = x1_ref[...] + buf[0] * rec[:, _R_W0:_R_W0 + 1] + buf[1] * rec[:, _R_W1:_R_W1 + 1]
    o_ref[...] = x * lax.rsqrt(jnp.mean(x * x, axis=-1, keepdims=True) + EPS) * nw_ref[...]


def _final(x1, rec, ys, e01, r01, pstart, final_norm_w, tm):
    t, d = x1.shape
    nw = final_norm_w.reshape(1, d)
    smem = pl.BlockSpec((2 * tm,), lambda i, ps: (i,), memory_space=pltpu.SMEM)
    return pl.pallas_call(
        _final_kernel,
        grid_spec=pltpu.PrefetchScalarGridSpec(
            num_scalar_prefetch=1, grid=(t // tm,),
            in_specs=[smem, smem, pl.BlockSpec((tm, d), lambda i, ps: (i, 0)),
                      pl.BlockSpec((tm, LANES), lambda i, ps: (i, 0)),
                      pl.BlockSpec((1, d), lambda i, ps: (0, 0)),
                      pl.BlockSpec(memory_space=pl.ANY)],
            out_specs=pl.BlockSpec((tm, d), lambda i, ps: (i, 0)),
            scratch_shapes=[pltpu.VMEM((2, tm, d), F32), pltpu.SemaphoreType.DMA(())]),
        out_shape=jax.ShapeDtypeStruct((t, d), F32),
        compiler_params=_cparams(("arbitrary",)),
        name="final",
    )(pstart, e01, r01, x1, rec, nw, ys)


def _moe_layout(counts):
    bm = MOE_ROWS
    padded = (counts + bm - 1) // bm * bm
    pends = jnp.cumsum(padded)
    pstart = pends - padded
    return pstart.astype(I32), pends.astype(I32)


def kernel(x, positions, attn_norm_w, w_in, hg_lower_bound, hg_out_norm_w, mla_q_norm_w, mla_w_uq, mla_kv_norm_w, mla_w_ukv, w_branch_hgrn, w_branch_mla, w_out, ffn_norm_w, router_group_w, router_group_b, router_expert_w, router_expert_b, expert_w1, expert_w3, expert_w2, final_norm_w):
    batch, seq, d = x.shape
    t = batch * seq
    depth = w_in.shape[0]
    assert d == 1024 and seq % 512 == 0 and t % 512 == 0
    x2 = x.reshape(t, d)
    for l in range(depth):
        q, kk, lf, v, g, cq, ckv, kr, sa, sb = _in_proj(x2, attn_norm_w[l], w_in[l], hg_lower_bound, l, tm=256)
        o_a = _hgrn(q, kk, lf, v, g, hg_out_norm_w[l], batch, seq)
        mq, mk, mv = _mla_prep(cq, ckv, kr, positions, mla_q_norm_w[l], mla_w_uq[l], mla_kv_norm_w[l],
                               mla_w_ukv[l], batch, seq, tm=512)
        o_b = _flash(mq, mk, mv, tq=256).reshape(t, MLA_HEADS * MLA_VDIM)
        x1, h2, rec, cnt = _combine(o_a, o_b, sa, sb, x2, w_branch_hgrn[l], w_branch_mla[l], w_out[l],
                                    ffn_norm_w[l], router_group_w[l], router_group_b[l],
                                    router_expert_w[l], router_expert_b[l], tm=256)

        counts = cnt[0, EXPERT_LANE0:EXPERT_LANE0 + N_EXPERTS].astype(I32)
        pstart, pends = _moe_layout(counts)
        n_blocks = (2 * t) // MOE_ROWS + N_EXPERTS
        cap = n_blocks * MOE_ROWS
        n_used = (pends[-1] // MOE_ROWS).reshape(1)
        blk = jnp.arange(n_blocks, dtype=I32)
        be = jnp.minimum(jnp.searchsorted(pends, blk * MOE_ROWS, side='right'), N_EXPERTS - 1).astype(I32)
        be = jnp.where(blk < n_used[0], be, be[n_used[0] - 1])
        e01 = rec[:, _R_E0:_R_E1 + 1].astype(I32).reshape(2 * t)
        r01 = rec[:, _R_RANK0:_R_RANK1 + 1].astype(I32).reshape(2 * t)

        xs = _dispatch(h2, e01, r01, pstart, cap, tm=512)
        ys = _experts(xs, expert_w1[l], expert_w3[l], expert_w2[l], be, n_used)
        assert l == depth - 1, "multi-layer stacking needs an un-normalised residual output"
        out = _final(x1, rec, ys, e01, r01, pstart, final_norm_w, tm=512)
    return out.reshape(batch, seq, d)
```

```python
import functools
import math

import numpy as np
import jax
import jax.numpy as jnp
from jax import lax
from jax.experimental import pallas as pl
from jax.experimental.pallas import tpu as pltpu

F32 = jnp.float32
BF16 = jnp.bfloat16
I32 = jnp.int32

EPS = 1e-6
LANES = 128
HG_HEADS = 8
HG_DIM = 128
MLA_HEADS = 8
MLA_Q_RANK = 384
MLA_KV_RANK = 256
MLA_NOPE = 128
MLA_ROPE = 64
MLA_VDIM = 128
MLA_QK_DIM = MLA_NOPE + MLA_ROPE
MLA_QK_PAD = 256
ROPE_THETA = 10000.0
N_GROUPS = 8
EXPERTS_PER_GROUP = 8
N_EXPERTS = N_GROUPS * EXPERTS_PER_GROUP
EXPERT_LANE0 = N_GROUPS
EXPERT_FF = 512

HG_CHUNK = 128
HG_LOW_BLOCKS = (1, 2, 4)
HG_HIGH_BLOCKS = (8, 16, 32, 64)
HG_HEADS_PER_STEP = 2
HG_CHUNK_UNROLL = 4
MOE_ROWS = 256
DMA_UNROLL = 8
VMEM_LIMIT = 56 * 1024 * 1024


def _cparams(sem):
    return pltpu.CompilerParams(dimension_semantics=sem, vmem_limit_bytes=VMEM_LIMIT)


def _dot(a, b):
    return jnp.dot(a, b, preferred_element_type=F32)


def _dot_nt(a, b):
    return lax.dot_general(a, b, (((1,), (1,)), ((), ())), preferred_element_type=F32)


def _dot_tn(a, b):
    return lax.dot_general(a, b, (((0,), (0,)), ((), ())), preferred_element_type=F32)


def _split_bf16(x):
    hi = x.astype(BF16)
    lo = (x - hi.astype(F32)).astype(BF16)
    return hi, lo


_IN_OFF_Q = 0
_IN_OFF_F = 1024
_IN_OFF_I = 2048
_IN_OFF_G = 3072
_IN_OFF_CQ = 4096
_IN_OFF_CKV = _IN_OFF_CQ + MLA_Q_RANK
_IN_OFF_KR = _IN_OFF_CKV + MLA_KV_RANK
_IN_OFF_GA = _IN_OFF_KR + LANES
_IN_OFF_GB = _IN_OFF_GA + 1024
_IN_COLS_PAD = _IN_OFF_GB + 1024


def _rope_lane_layout(w):
    half = MLA_ROPE // 2
    z = jnp.zeros((w.shape[0], half), w.dtype)
    return jnp.concatenate([w[:, :half], z, w[:, half:], z], axis=1)


def _inproj_kernel(x_ref, nw_ref, lbt_ref, w_ref, q_ref, kk_ref, lf_ref, v_ref, g_ref, cq_ref, ckv_ref,
                   kr_ref, sa_ref, sb_ref, h_scr, *, layer):
    x = x_ref[...]
    ms = jnp.mean(x * x, axis=-1, keepdims=True)
    h_scr[...] = (x * lax.rsqrt(ms + EPS) * nw_ref[...]).astype(BF16)

    tab = lbt_ref[...]
    e = jnp.exp(tab - jnp.max(tab, axis=0, keepdims=True))
    sm = e / jnp.sum(e, axis=0, keepdims=True)
    lb = jnp.sum(sm[:layer + 1], axis=0, keepdims=True)

    def proj(off, width):
        return _dot(h_scr[...], w_ref[:, off:off + width])

    half = 512
    for c in range(0, 1024, half):
        z = proj(_IN_OFF_Q + c, half)
        q_ref[:, c:c + half] = (z * jax.nn.sigmoid(z) * (HG_DIM ** -0.5)).astype(BF16)
        z = proj(_IN_OFF_F + c, half)
        lbc = lb[:, c:c + half]
        fg = lbc + (1.0 - lbc) * jax.nn.sigmoid(z)
        lf_ref[:, c:c + half] = jnp.log(fg)
        kk_ref[:, c:c + half] = (1.0 - fg).astype(BF16)
        v_ref[:, c:c + half] = proj(_IN_OFF_I + c, half).astype(BF16)
        z = proj(_IN_OFF_G + c, half)
        g_ref[:, c:c + half] = (z * jax.nn.sigmoid(z)).astype(BF16)
        sa_ref[:, c:c + half] = jax.nn.sigmoid(proj(_IN_OFF_GA + c, half)).astype(BF16)
        sb_ref[:, c:c + half] = jax.nn.sigmoid(proj(_IN_OFF_GB + c, half)).astype(BF16)
    cq_ref[...] = proj(_IN_OFF_CQ, MLA_Q_RANK)
    ckv_ref[...] = proj(_IN_OFF_CKV, MLA_KV_RANK)
    kr_ref[...] = proj(_IN_OFF_KR, LANES)


def _in_proj(x2, attn_norm_w, w_in, lb_table, layer, tm):
    t, d = x2.shape
    parts = []
    off = 0
    for width in (1024, 1024, 1024, 1024, MLA_Q_RANK, MLA_KV_RANK, MLA_ROPE, 1024, 1024):
        parts.append(w_in[:, off:off + width])
        off += width
    parts[6] = _rope_lane_layout(parts[6])
    w = jnp.concatenate(parts, axis=1).astype(BF16)
    assert w.shape[1] == _IN_COLS_PAD
    row = lambda width: pl.BlockSpec((tm, width), lambda i: (i, 0))
    full = lambda a: pl.BlockSpec(a.shape, lambda i: (0,) * a.ndim)
    nw = attn_norm_w.reshape(1, d)
    out_shape = [jax.ShapeDtypeStruct((t, 1024), BF16),
                 jax.ShapeDtypeStruct((t, 1024), BF16),
                 jax.ShapeDtypeStruct((t, 1024), F32),
                 jax.ShapeDtypeStruct((t, 1024), BF16),
                 jax.ShapeDtypeStruct((t, 1024), BF16),
                 jax.ShapeDtypeStruct((t, MLA_Q_RANK), F32),
                 jax.ShapeDtypeStruct((t, MLA_KV_RANK), F32),
                 jax.ShapeDtypeStruct((t, LANES), F32),
                 jax.ShapeDtypeStruct((t, 1024), BF16),
                 jax.ShapeDtypeStruct((t, 1024), BF16)]
    return pl.pallas_call(
        functools.partial(_inproj_kernel, layer=layer),
        grid=(t // tm,),
        in_specs=[row(d), full(nw), full(lb_table),
                  pl.BlockSpec(w.shape, lambda i: (0, 0), pipeline_mode=pl.Buffered(1))],
        out_specs=[row(s.shape[1]) for s in out_shape],
        out_shape=out_shape,
        scratch_shapes=[pltpu.VMEM((tm, d), BF16)],
        compiler_params=_cparams(("parallel",)),
        name="in_proj",
    )(x2, nw, lb_table, w)


def _hgrn_tables():
    n = HG_CHUNK
    r = np.arange(n)[:, None]
    j = np.arange(n)[None, :]
    mats = [j <= r]
    for m in HG_LOW_BLOCKS:
        c = (r // (2 * m)) * (2 * m) + m - 1
        odd = (r & m) != 0
        mats.append(np.where(odd, (j > c) & (j <= r), (j > r) & (j <= c)))
    tab = np.concatenate(mats, axis=0).astype(np.float32)
    return jnp.asarray(np.concatenate([tab, tab], axis=1), dtype=BF16)


def _hgrn_kernel(q_ref, kk_ref, lf_ref, v_ref, g_ref, nw_ref, tab_ref, o_ref, st_ref, *, heads, n_chunks):
    n = HG_CHUNK
    assert n == HG_DIM
    row = lax.broadcasted_iota(I32, (n, n), 0)
    col = lax.broadcasted_iota(I32, (n, n), 1)
    code = jnp.where(col <= row, 32 - lax.clz(row ^ col), -1)
    st_ref[...] = jnp.zeros_like(st_ref)
    nw = nw_ref[...]

    def chunk(c, carry):
        r0 = pl.multiple_of(c * n, n)
        for h in range(heads):
            ls = slice(h * HG_DIM, (h + 1) * HG_DIM)
            lf = lf_ref[pl.ds(r0, n), ls]
            qb = q_ref[pl.ds(r0, n), ls]
            kb = kk_ref[pl.ds(r0, n), ls]
            vb = v_ref[pl.ds(r0, n), ls]
            q = qb.astype(F32)
            k = kb.astype(F32)
            hi, lo = _split_bf16(lf)
            ex = _dot(tab_ref[...], jnp.concatenate([hi, lo], axis=0))
            b = ex[0:n, :]

            a = jnp.where(code == 0, _dot_nt(qb, kb), 0.0)
            for i, m in enumerate(HG_LOW_BLOCKS):
                odd = (row & m) != 0
                p = jnp.where(odd, q, k) * jnp.exp(ex[(i + 1) * n:(i + 2) * n, :])
                qp = jnp.where(odd, p, 0.0).astype(BF16)
                kp = jnp.where(odd, 0.0, p).astype(BF16)
                a = jnp.where(code == m.bit_length(), _dot_nt(qp, kp), a)
            for m in HG_HIGH_BLOCKS:
                bases = range(0, n, 2 * m)
                e_parts, s_parts = [], []
                for base in bases:
                    mid = b[base + m - 1:base + m, :]
                    e_parts += [mid - b[base:base + m, :], b[base + m:base + 2 * m, :] - mid]
                    s_parts += [k[base:base + m, :], q[base + m:base + 2 * m, :]]
                p = jnp.concatenate(s_parts, axis=0) * jnp.exp(jnp.concatenate(e_parts, axis=0))
                zero = jnp.zeros((m, HG_DIM), F32)
                kp = jnp.concatenate([x for base in bases for x in (p[base:base + m, :], zero)], axis=0)
                qc = jnp.concatenate([p[base + m:base + 2 * m, :] for base in bases], axis=0)
                r = _dot_nt(qc.astype(BF16), kp.astype(BF16))
                pieces = []
                for j, base in enumerate(bases):
                    second = slice(base + m, base + 2 * m)
                    pieces += [a[base:base + m, :],
                               jnp.where(code[second, :] == m.bit_length(), r[j * m:(j + 1) * m, :], a[second, :])]
                a = jnp.concatenate(pieces, axis=0)

            st = st_ref[h]
            b_last = b[n - 1:n, :]
            o = _dot(a.astype(BF16), vb) + _dot_nt((q * jnp.exp(b)).astype(BF16), st.astype(BF16))
            kdec = (k * jnp.exp(b_last - b)).astype(BF16)
            st_ref[h] = st * jnp.exp(b_last) + _dot_tn(vb, kdec)

            ms = jnp.mean(o * o, axis=-1, keepdims=True)
            y = o * lax.rsqrt(ms + EPS) * nw
            o_ref[pl.ds(r0, n), ls] = (y * g_ref[pl.ds(r0, n), ls].astype(F32)).astype(BF16)
        return carry

    lax.fori_loop(0, n_chunks, chunk, 0, unroll=HG_CHUNK_UNROLL)


def _hgrn(q, kk, lf, v, g, out_norm_w, batch, seq):
    t, width = q.shape
    hp = HG_HEADS_PER_STEP
    wblk = hp * HG_DIM
    tab = _hgrn_tables()
    spec = pl.BlockSpec((seq, wblk), lambda b, j: (b, j))
    full = lambda a: pl.BlockSpec(a.shape, lambda b, j: (0,) * a.ndim)
    nw = out_norm_w.reshape(1, HG_DIM)
    return pl.pallas_call(
        functools.partial(_hgrn_kernel, heads=hp, n_chunks=seq // HG_CHUNK),
        grid=(batch, width // wblk),
        in_specs=[spec, spec, spec, spec, spec, full(nw), full(tab)],
        out_specs=spec,
        out_shape=jax.ShapeDtypeStruct((t, width), BF16),
        scratch_shapes=[pltpu.VMEM((hp, HG_DIM, HG_DIM), F32)],
        compiler_params=_cparams(("parallel", "parallel")),
        name="hgrn",
    )(q, kk, lf, v, g, nw, tab)


def _rope(p, cos_t, sin_t):
    return p * cos_t + pltpu.roll(p, LANES // 2, axis=1) * sin_t


def _mla_prep_kernel(cq_ref, ckv_ref, kr_ref, pos_ref, qnw_ref, kvnw_ref, wq_ref, wkv_ref, rot_ref,
                     q_ref, k_ref, v_ref):
    ang = pos_ref[...] * rot_ref[0:1, :]
    cos_t = jnp.cos(ang) * rot_ref[1:2, :]
    sin_t = jnp.sin(ang) * rot_ref[2:3, :]

    cq = cq_ref[...]
    cqn = cq * lax.rsqrt(jnp.mean(cq * cq, axis=-1, keepdims=True) + EPS) * qnw_ref[...]
    ckv = ckv_ref[...]
    ckvn = ckv * lax.rsqrt(jnp.mean(ckv * ckv, axis=-1, keepdims=True) + EPS) * kvnw_ref[...]
    cqn = cqn.astype(BF16)
    ckvn = ckvn.astype(BF16)
    k_pe = _rope(kr_ref[...], cos_t, sin_t).astype(BF16)
    scale = MLA_QK_DIM ** -0.5
    for h in range(MLA_HEADS):
        c0 = h * MLA_QK_PAD
        qh = _dot(cqn, wq_ref[:, c0:c0 + MLA_QK_PAD])
        q_ref[h, :, 0:MLA_NOPE] = (qh[:, :MLA_NOPE] * scale).astype(BF16)
        q_ref[h, :, MLA_NOPE:MLA_QK_PAD] = (_rope(qh[:, MLA_NOPE:], cos_t, sin_t) * scale).astype(BF16)
        kvh = _dot(ckvn, wkv_ref[:, c0:c0 + MLA_NOPE + MLA_VDIM])
        k_ref[h, :, 0:MLA_NOPE] = kvh[:, :MLA_NOPE].astype(BF16)
        k_ref[h, :, MLA_NOPE:MLA_QK_PAD] = k_pe
        v_ref[h] = kvh[:, MLA_NOPE:].astype(BF16)


def _mla_prep(cq, ckv, kr, positions, q_norm_w, w_uq, kv_norm_w, w_ukv, batch, seq, tm):
    t = cq.shape[0]
    half = MLA_ROPE // 2
    wq = w_uq.reshape(MLA_Q_RANK, MLA_HEADS, MLA_QK_DIM)
    zq = jnp.zeros((MLA_Q_RANK, MLA_HEADS, half), w_uq.dtype)
    wq = jnp.concatenate([wq[:, :, :MLA_NOPE], wq[:, :, MLA_NOPE:MLA_NOPE + half], zq,
                          wq[:, :, MLA_NOPE + half:], zq], axis=2)
    wq = wq.reshape(MLA_Q_RANK, MLA_HEADS * MLA_QK_PAD).astype(BF16)
    wkv = w_ukv.astype(BF16)
    inv_freq = ROPE_THETA ** (-jnp.arange(half, dtype=F32) / half)
    z = jnp.zeros((half,), F32)
    o = jnp.ones((half,), F32)
    rot = jnp.stack([jnp.concatenate([inv_freq, z, inv_freq, z]),
                     jnp.concatenate([o, z, o, z]),
                     jnp.concatenate([-o, z, o, z])])
    pos = positions.astype(F32).reshape(t, 1)
    nblk = seq // tm
    row = lambda width: pl.BlockSpec((tm, width), lambda i: (i, 0))
    full = lambda a: pl.BlockSpec(a.shape, lambda i: (0,) * a.ndim)
    hspec = lambda width: pl.BlockSpec((None, MLA_HEADS, tm, width), lambda i: (i // nblk, 0, i % nblk, 0))
    qnw = q_norm_w.reshape(1, -1)
    kvnw = kv_norm_w.reshape(1, -1)
    return pl.pallas_call(
        _mla_prep_kernel,
        grid=(t // tm,),
        in_specs=[row(MLA_Q_RANK), row(MLA_KV_RANK), row(LANES), row(1), full(qnw), full(kvnw),
                  full(wq), full(wkv), full(rot)],
        out_specs=[hspec(MLA_QK_PAD), hspec(MLA_QK_PAD), hspec(MLA_VDIM)],
        out_shape=[jax.ShapeDtypeStruct((batch, MLA_HEADS, seq, MLA_QK_PAD), BF16),
                   jax.ShapeDtypeStruct((batch, MLA_HEADS, seq, MLA_QK_PAD), BF16),
                   jax.ShapeDtypeStruct((batch, MLA_HEADS, seq, MLA_VDIM), BF16)],
        compiler_params=_cparams(("parallel",)),
        name="mla_prep",
    )(cq, ckv, kr, pos, qnw, kvnw, wq, wkv, rot)


def _flash_kernel(q_ref, k_ref, v_ref, o_ref, *, tq):
    seq = q_ref.shape[0]
    row = lax.broadcasted_iota(I32, (tq, tq), 0)
    col = lax.broadcasted_iota(I32, (tq, tq), 1)
    causal = row >= col
    for i in range(seq // tq):
        lo, hi = i * tq, (i + 1) * tq
        q = q_ref[lo:hi, :]
        sd = jnp.where(causal, _dot_nt(q, k_ref[lo:hi, :]), -jnp.inf)
        m = jnp.max(sd, axis=-1, keepdims=True)
        if i > 0:
            so = _dot_nt(q, k_ref[0:lo, :])
            m = jnp.maximum(m, jnp.max(so, axis=-1, keepdims=True))
            po = jnp.exp(so - m)
            den = jnp.sum(po, axis=-1, keepdims=True)
            acc = _dot(po.astype(BF16), v_ref[0:lo, :])
        pd = jnp.exp(sd - m)
        if i > 0:
            den = den + jnp.sum(pd, axis=-1, keepdims=True)
            acc = acc + _dot(pd.astype(BF16), v_ref[lo:hi, :])
        else:
            den = jnp.sum(pd, axis=-1, keepdims=True)
            acc = _dot(pd.astype(BF16), v_ref[lo:hi, :])
        o_ref[lo:hi, :] = (acc / den).astype(BF16)


def _flash(q, k, v, tq):
    batch, heads, seq, _ = q.shape
    qk = pl.BlockSpec((None, None, seq, MLA_QK_PAD), lambda b, h: (b, h, 0, 0))
    return pl.pallas_call(
        functools.partial(_flash_kernel, tq=tq),
        grid=(batch, heads),
        in_specs=[qk, qk, pl.BlockSpec((None, None, seq, MLA_VDIM), lambda b, h: (b, h, 0, 0))],
        out_specs=pl.BlockSpec((None, seq, MLA_VDIM), lambda b, h: (b, 0, h)),
        out_shape=jax.ShapeDtypeStruct((batch, seq, heads * MLA_VDIM), BF16),
        compiler_params=_cparams(("parallel", "parallel")),
        name="flash",
    )(q, k, v)


_R_E0, _R_E1, _R_W0, _R_W1, _R_RANK0, _R_RANK1 = 0, 1, 2, 3, 4, 5


def _first_lane_equal(x, value, lane):
    return jnp.min(jnp.where(x == value, lane, LANES), axis=-1, keepdims=True)


def _combine_kernel(oa_ref, ob_ref, sa_ref, sb_ref, x_ref, wa_ref, wb_ref, wo_ref, fnw_ref, wr_ref, br_ref,
                    lt_ref, x1_ref, h2_ref, rec_ref, cnt_ref, run_ref):
    i = pl.program_id(0)

    @pl.when(i == 0)
    def _():
        run_ref[...] = jnp.zeros_like(run_ref)

    ya = _dot(oa_ref[...], wa_ref[...])
    yb = _dot(ob_ref[...], wb_ref[...])
    y = sa_ref[...].astype(F32) * ya + sb_ref[...].astype(F32) * yb
    x1 = x_ref[...] + _dot(y.astype(BF16), wo_ref[...])
    x1_ref[...] = x1
    h2 = x1 * lax.rsqrt(jnp.mean(x1 * x1, axis=-1, keepdims=True) + EPS) * fnw_ref[...]
    h2_ref[...] = h2

    hh, hl = _split_bf16(h2)
    logits = _dot(hh, wr_ref[0]) + _dot(hl, wr_ref[0]) + _dot(hh, wr_ref[1]) + br_ref[...]
    tm = logits.shape[0]
    lane = lax.broadcasted_iota(I32, (tm, LANES), 1)
    neg = -jnp.inf

    gl = jnp.where(lane < N_GROUPS, logits, neg)
    ge = jnp.exp(gl - jnp.max(gl, axis=-1, keepdims=True))
    gp = ge / jnp.sum(ge, axis=-1, keepdims=True)
    g_w = jnp.max(gp, axis=-1, keepdims=True)
    g_idx = _first_lane_equal(gp, g_w, lane)

    ej = lane - EXPERT_LANE0
    sel = (ej >= 0) & (ej < N_EXPERTS) & ((ej >> 3) == g_idx)
    el = jnp.where(sel, logits, neg)
    ee = jnp.exp(el - jnp.max(el, axis=-1, keepdims=True))
    ep = jnp.where(sel, ee / jnp.sum(ee, axis=-1, keepdims=True), -1.0)
    p0 = jnp.max(ep, axis=-1, keepdims=True)
    l0 = _first_lane_equal(ep, p0, lane)
    ep1 = jnp.where(lane == l0, -1.0, ep)
    p1 = jnp.max(ep1, axis=-1, keepdims=True)
    l1 = _first_lane_equal(ep1, p1, lane)
    psum = p0 + p1
    w0 = g_w * (p0 / psum)
    w1 = g_w * (p1 / psum)

    oh0 = lane == l0
    oh1 = lane == l1
    oh = jnp.where(oh0 | oh1, 1.0, 0.0)
    before = run_ref[...] + _dot(lt_ref[...], oh.astype(BF16))
    rank0 = jnp.sum(jnp.where(oh0, before, 0.0), axis=-1, keepdims=True)
    rank1 = jnp.sum(jnp.where(oh1, before, 0.0), axis=-1, keepdims=True)
    run = run_ref[...] + jnp.sum(oh, axis=0, keepdims=True)
    run_ref[...] = run
    cnt_ref[...] = run

    rec = jnp.where(lane == _R_E0, (l0 - EXPERT_LANE0).astype(F32), 0.0)
    rec = jnp.where(lane == _R_E1, (l1 - EXPERT_LANE0).astype(F32), rec)
    rec = jnp.where(lane == _R_W0, w0, rec)
    rec = jnp.where(lane == _R_W1, w1, rec)
    rec = jnp.where(lane == _R_RANK0, rank0, rec)
    rec = jnp.where(lane == _R_RANK1, rank1, rec)
    rec_ref[...] = rec


def _combine(o_a, o_b, sa, sb, x2, w_a, w_b, w_o, ffn_norm_w, w_group, b_group, w_expert, b_expert, tm):
    t, d = x2.shape
    pad = LANES - N_GROUPS - N_EXPERTS
    wr = jnp.concatenate([w_group, w_expert, jnp.zeros((d, pad), F32)], axis=1)
    wr_hi = wr.astype(BF16)
    wr_lo = (wr - wr_hi.astype(F32)).astype(BF16)
    wr2 = jnp.stack([wr_hi, wr_lo])
    br = jnp.concatenate([b_group, b_expert, jnp.zeros((pad,), F32)]).reshape(1, LANES)
    lt = jnp.asarray(np.tril(np.ones((tm, tm), np.float32), -1), dtype=BF16)
    fnw = ffn_norm_w.reshape(1, d)
    row = lambda width: pl.BlockSpec((tm, width), lambda i: (i, 0))
    full = lambda a: pl.BlockSpec(a.shape, lambda i: (0,) * a.ndim)
    wa, wb, wo = w_a.astype(BF16), w_b.astype(BF16), w_o.astype(BF16)
    return pl.pallas_call(
        _combine_kernel,
        grid=(t // tm,),
        in_specs=[row(d), row(d), row(d), row(d), row(d), full(wa), full(wb), full(wo), full(fnw),
                  full(wr2), full(br), full(lt)],
        out_specs=[row(d), row(d), row(LANES), pl.BlockSpec((1, LANES), lambda i: (0, 0))],
        out_shape=[jax.ShapeDtypeStruct((t, d), F32),
                   jax.ShapeDtypeStruct((t, d), F32),
                   jax.ShapeDtypeStruct((t, LANES), F32),
                   jax.ShapeDtypeStruct((1, LANES), F32)],
        scratch_shapes=[pltpu.VMEM((1, LANES), F32)],
        compiler_params=_cparams(("arbitrary",)),
        name="combine",
    )(o_a, o_b, sa, sb, x2, wa, wb, wo, fnw, wr2, br, lt)


def _row_copy(src_ref, src_row, dst_ref, dst_row, sem):
    return pltpu.make_async_copy(src_ref.at[pl.ds(src_row, 1)], dst_ref.at[pl.ds(dst_row, 1)], sem)


def _dest_kernel(rec_ref, ps_ref, d_ref):
    rec = rec_ref[...]
    lane = lax.broadcasted_iota(I32, rec.shape, 1)
    ps = ps_ref[...]

    def lookup(e_lane, r_lane):
        e = rec[:, e_lane:e_lane + 1].astype(I32) + EXPERT_LANE0
        start = jnp.sum(jnp.where(lane == e, ps, 0.0), axis=-1, keepdims=True)
        return (start + rec[:, r_lane:r_lane + 1]).astype(I32)

    d0 = lookup(_R_E0, _R_RANK0)
    d1 = lookup(_R_E1, _R_RANK1)
    d_ref[...] = jnp.where(lane == 0, d0, jnp.where(lane == 1, d1, 0))


def _dest(rec, pstart, tm):
    t = rec.shape[0]
    ps = jnp.zeros((1, LANES), F32).at[0, EXPERT_LANE0:EXPERT_LANE0 + N_EXPERTS].set(pstart.astype(F32))
    return pl.pallas_call(
        _dest_kernel,
        grid=(t // tm,),
        in_specs=[pl.BlockSpec((tm, LANES), lambda i: (i, 0)), pl.BlockSpec((1, LANES), lambda i: (0, 0))],
        out_specs=pl.BlockSpec((tm, LANES), lambda i: (i, 0)),
        out_shape=jax.ShapeDtypeStruct((t, LANES), I32),
        compiler_params=_cparams(("parallel",)),
        name="dest",
    )(rec, ps)


def _dispatch_kernel(d_ref, h_ref, xs_in_ref, xs_ref, sem):
    del xs_in_ref
    tm = h_ref.shape[0]

    def start(j, carry):
        for u in range(DMA_UNROLL):
            tok = j * DMA_UNROLL + u
            for k in range(2):
                _row_copy(h_ref, tok, xs_ref, d_ref[2 * tok + k], sem).start(priority=k)
        return carry

    lax.fori_loop(0, tm // DMA_UNROLL, start, 0)

    def wait(j, carry):
        for _ in range(2 * DMA_UNROLL):
            _row_copy(h_ref, 0, xs_ref, 0, sem).wait()
        return carry

    lax.fori_loop(0, tm // DMA_UNROLL, wait, 0)


def _dispatch(h2, dest, cap, tm):
    t, d = h2.shape
    xs0 = jnp.zeros((cap, d), h2.dtype)
    return pl.pallas_call(
        _dispatch_kernel,
        grid=(t // tm,),
        in_specs=[pl.BlockSpec((2 * tm,), lambda i: (i,), memory_space=pltpu.SMEM),
                  pl.BlockSpec((tm, d), lambda i: (i, 0)),
                  pl.BlockSpec(memory_space=pl.ANY)],
        out_specs=pl.BlockSpec(memory_space=pl.ANY),
        out_shape=jax.ShapeDtypeStruct((cap, d), h2.dtype),
        scratch_shapes=[pltpu.SemaphoreType.DMA(())],
        input_output_aliases={2: 0},
        compiler_params=_cparams(("arbitrary",)),
        name="dispatch",
    )(dest, h2, xs0)


def _experts_kernel(be_ref, nu_ref, xs_ref, w1_ref, w3_ref, w2_ref, ys_ref, w1b, w3b, w2b):
    i = pl.program_id(0)

    @pl.when(i < nu_ref[0])
    def _():
        changed = jnp.logical_or(i == 0, be_ref[i] != be_ref[jnp.maximum(i - 1, 0)])

        @pl.when(changed)
        def _():
            w1b[...] = w1_ref[...].astype(BF16)
            w3b[...] = w3_ref[...].astype(BF16)
            w2b[...] = w2_ref[...].astype(BF16)

        x = xs_ref[...].astype(BF16)
        h1 = _dot(x, w1b[...])
        h3 = _dot(x, w3b[...])
        hid = h1 * jax.nn.sigmoid(h1) * h3
        ys_ref[...] = _dot(hid.astype(BF16), w2b[...])

    @pl.when(i >= nu_ref[0])
    def _():
        ys_ref[...] = jnp.zeros_like(ys_ref)


def _experts(xs, w1, w3, w2, block_expert, n_used):
    cap, d = xs.shape
    bm = MOE_ROWS
    ff = w1.shape[2]
    rows = pl.BlockSpec((bm, d), lambda i, be, nu: (jnp.minimum(i, nu[0] - 1), 0))
    return pl.pallas_call(
        _experts_kernel,
        grid_spec=pltpu.PrefetchScalarGridSpec(
            num_scalar_prefetch=2, grid=(cap // bm,),
            in_specs=[rows,
                      pl.BlockSpec((None, d, ff), lambda i, be, nu: (be[i], 0, 0)),
                      pl.BlockSpec((None, d, ff), lambda i, be, nu: (be[i], 0, 0)),
                      pl.BlockSpec((None, ff, d), lambda i, be, nu: (be[i], 0, 0))],
            out_specs=pl.BlockSpec((bm, d), lambda i, be, nu: (i, 0)),
            scratch_shapes=[pltpu.VMEM((d, ff), BF16), pltpu.VMEM((d, ff), BF16), pltpu.VMEM((ff, d), BF16)]),
        out_shape=jax.ShapeDtypeStruct((cap, d), F32),
        compiler_params=_cparams(("arbitrary",)),
        name="experts",
    )(block_expert, n_used, xs, w1, w3, w2)


def _final_kernel(d_ref, x1_ref, rec_ref, nw_ref, ys_ref, o_ref, buf, sem):
    tm = x1_ref.shape[0]

    def start(j, carry):
        for u in range(DMA_UNROLL):
            tok = j * DMA_UNROLL + u
            for k in range(2):
                _row_copy(ys_ref, d_ref[2 * tok + k], buf.at[k], tok, sem).start(priority=k)
        return carry

    lax.fori_loop(0, tm // DMA_UNROLL, start, 0)

    def wait(j, carry):
        for _ in range(2 * DMA_UNROLL):
            _row_copy(ys_ref, 0, buf.at[0], 0, sem).wait()
        return carry

    lax.fori_loop(0, tm // DMA_UNROLL, wait, 0)

    rec = rec_ref[...]
    x = x1_ref[...] + buf[0] * rec[:, _R_W0:_R_W0 + 1] + buf[1] * rec[:, _R_W1:_R_W1 + 1]
    o_ref[...] = x * lax.rsqrt(jnp.mean(x * x, axis=-1, keepdims=True) + EPS) * nw_ref[...]


def _final(x1, rec, ys, dest, final_norm_w, tm):
    t, d = x1.shape
    nw = final_norm_w.reshape(1, d)
    return pl.pallas_call(
        _final_kernel,
        grid=(t // tm,),
        in_specs=[pl.BlockSpec((2 * tm,), lambda i: (i,), memory_space=pltpu.SMEM),
                  pl.BlockSpec((tm, d), lambda i: (i, 0)),
                  pl.BlockSpec((tm, LANES), lambda i: (i, 0)),
                  pl.BlockSpec((1, d), lambda i: (0, 0)),
                  pl.BlockSpec(memory_space=pl.ANY)],
        out_specs=pl.BlockSpec((tm, d), lambda i: (i, 0)),
        out_shape=jax.ShapeDtypeStruct((t, d), F32),
        scratch_shapes=[pltpu.VMEM((2, tm, d), F32), pltpu.SemaphoreType.DMA(())],
        compiler_params=_cparams(("arbitrary",)),
        name="final",
    )(dest, x1, rec, nw, ys)


def _moe_layout(counts):
    bm = MOE_ROWS
    padded = (counts + bm - 1) // bm * bm
    pends = jnp.cumsum(padded)
    pstart = pends - padded
    return pstart.astype(I32), pends.astype(I32)


def kernel(x, positions, attn_norm_w, w_in, hg_lower_bound, hg_out_norm_w, mla_q_norm_w, mla_w_uq, mla_kv_norm_w, mla_w_ukv, w_branch_hgrn, w_branch_mla, w_out, ffn_norm_w, router_group_w, router_group_b, router_expert_w, router_expert_b, expert_w1, expert_w3, expert_w2, final_norm_w):
    batch, seq, d = x.shape
    t = batch * seq
    depth = w_in.shape[0]
    assert d == 1024 and seq % 512 == 0 and t % 512 == 0
    x2 = x.reshape(t, d)
    for l in range(depth):
        q, kk, lf, v, g, cq, ckv, kr, sa, sb = _in_proj(x2, attn_norm_w[l], w_in[l], hg_lower_bound, l, tm=256)
        o_a = _hgrn(q, kk, lf, v, g, hg_out_norm_w[l], batch, seq)
        mq, mk, mv = _mla_prep(cq, ckv, kr, positions, mla_q_norm_w[l], mla_w_uq[l], mla_kv_norm_w[l],
                               mla_w_ukv[l], batch, seq, tm=512)
        o_b = _flash(mq, mk, mv, tq=256).reshape(t, MLA_HEADS * MLA_VDIM)
        x1, h2, rec, cnt = _combine(o_a, o_b, sa, sb, x2, w_branch_hgrn[l], w_branch_mla[l], w_out[l],
                                    ffn_norm_w[l], router_group_w[l], router_group_b[l],
                                    router_expert_w[l], router_expert_b[l], tm=256)

        counts = cnt[0, EXPERT_LANE0:EXPERT_LANE0 + N_EXPERTS].astype(I32)
        pstart, pends = _moe_layout(counts)
        n_blocks = (2 * t) // MOE_ROWS + N_EXPERTS
        cap = n_blocks * MOE_ROWS
        n_used = (pends[-1] // MOE_ROWS).reshape(1)
        blk = jnp.minimum(jnp.arange(n_blocks, dtype=I32), n_used[0] - 1)
        be = jnp.sum((blk[:, None] * MOE_ROWS >= pends[None, :]).astype(I32), axis=1)
        dest = _dest(rec, pstart, tm=512)[:, 0:2].reshape(2 * t)

        xs = _dispatch(h2, dest, cap, tm=512)
        ys = _experts(xs, expert_w1[l], expert_w3[l], expert_w2[l], be, n_used)
        assert l == depth - 1, "multi-layer stacking needs an un-normalised residual output"
        out = _final(x1, rec, ys, dest, final_norm_w, tm=512)
    return out.reshape(batch, seq, d)
```

```python
import functools
import math

import numpy as np
import jax
import jax.numpy as jnp
from jax import lax
from jax.experimental import pallas as pl
from jax.experimental.pallas import tpu as pltpu

F32 = jnp.float32
BF16 = jnp.bfloat16
I32 = jnp.int32

EPS = 1e-6
LANES = 128
HG_HEADS = 8
HG_DIM = 128
MLA_HEADS = 8
MLA_Q_RANK = 384
MLA_KV_RANK = 256
MLA_NOPE = 128
MLA_ROPE = 64
MLA_VDIM = 128
MLA_QK_DIM = MLA_NOPE + MLA_ROPE
MLA_QK_PAD = 256
ROPE_THETA = 10000.0
N_GROUPS = 8
EXPERTS_PER_GROUP = 8
N_EXPERTS = N_GROUPS * EXPERTS_PER_GROUP
EXPERT_LANE0 = N_GROUPS
EXPERT_FF = 512

HG_CHUNK = 128
HG_LOW_BLOCKS = (1, 2, 4)
HG_HIGH_BLOCKS = (8, 16, 32, 64)
HG_HEADS_PER_STEP = 2
HG_CHUNK_UNROLL = 8
COMBINE_SUB = 256
MOE_ROWS = 256
MOE_ROW_GROUPS = 2
DMA_UNROLL = 8
VMEM_LIMIT = 56 * 1024 * 1024


def _cparams(sem):
    return pltpu.CompilerParams(dimension_semantics=sem, vmem_limit_bytes=VMEM_LIMIT)


def _dot(a, b):
    return jnp.dot(a, b, preferred_element_type=F32)


def _dot_nt(a, b):
    return lax.dot_general(a, b, (((1,), (1,)), ((), ())), preferred_element_type=F32)


def _dot_tn(a, b):
    return lax.dot_general(a, b, (((0,), (0,)), ((), ())), preferred_element_type=F32)


def _split_bf16(x):
    hi = x.astype(BF16)
    lo = (x - hi.astype(F32)).astype(BF16)
    return hi, lo


_IN_OFF_Q = 0
_IN_OFF_F = 1024
_IN_OFF_I = 2048
_IN_OFF_G = 3072
_IN_OFF_CQ = 4096
_IN_OFF_CKV = _IN_OFF_CQ + MLA_Q_RANK
_IN_OFF_KR = _IN_OFF_CKV + MLA_KV_RANK
_IN_OFF_GA = _IN_OFF_KR + LANES
_IN_OFF_GB = _IN_OFF_GA + 1024
_IN_COLS_PAD = _IN_OFF_GB + 1024


def _rope_lane_layout(w):
    half = MLA_ROPE // 2
    z = jnp.zeros((w.shape[0], half), w.dtype)
    return jnp.concatenate([w[:, :half], z, w[:, half:], z], axis=1)


def _inproj_kernel(x_ref, nw_ref, lbt_ref, w_ref, q_ref, kk_ref, lf_ref, v_ref, g_ref, cq_ref, ckv_ref,
                   kr_ref, sa_ref, sb_ref, h_scr, *, layer):
    x = x_ref[...]
    ms = jnp.mean(x * x, axis=-1, keepdims=True)
    h_scr[...] = (x * lax.rsqrt(ms + EPS) * nw_ref[...]).astype(BF16)

    tab = lbt_ref[...]
    e = jnp.exp(tab - jnp.max(tab, axis=0, keepdims=True))
    sm = e / jnp.sum(e, axis=0, keepdims=True)
    lb = jnp.sum(sm[:layer + 1], axis=0, keepdims=True)

    def proj(off, width):
        return _dot(h_scr[...], w_ref[:, off:off + width])

    half = 512
    for c in range(0, 1024, half):
        z = proj(_IN_OFF_Q + c, half)
        q_ref[:, c:c + half] = (z * jax.nn.sigmoid(z) * (HG_DIM ** -0.5)).astype(BF16)
        z = proj(_IN_OFF_F + c, half)
        lbc = lb[:, c:c + half]
        fg = lbc + (1.0 - lbc) * jax.nn.sigmoid(z)
        lf_ref[:, c:c + half] = jnp.log(fg)
        kk_ref[:, c:c + half] = (1.0 - fg).astype(BF16)
        v_ref[:, c:c + half] = proj(_IN_OFF_I + c, half).astype(BF16)
        z = proj(_IN_OFF_G + c, half)
        g_ref[:, c:c + half] = (z * jax.nn.sigmoid(z)).astype(BF16)
        sa_ref[:, c:c + half] = jax.nn.sigmoid(proj(_IN_OFF_GA + c, half)).astype(BF16)
        sb_ref[:, c:c + half] = jax.nn.sigmoid(proj(_IN_OFF_GB + c, half)).astype(BF16)
    cq_ref[...] = proj(_IN_OFF_CQ, MLA_Q_RANK)
    ckv_ref[...] = proj(_IN_OFF_CKV, MLA_KV_RANK)
    kr_ref[...] = proj(_IN_OFF_KR, LANES)


def _in_proj(x2, attn_norm_w, w_in, lb_table, layer, tm):
    t, d = x2.shape
    parts = []
    off = 0
    for width in (1024, 1024, 1024, 1024, MLA_Q_RANK, MLA_KV_RANK, MLA_ROPE, 1024, 1024):
        parts.append(w_in[:, off:off + width])
        off += width
    parts[6] = _rope_lane_layout(parts[6])
    w = jnp.concatenate(parts, axis=1).astype(BF16)
    assert w.shape[1] == _IN_COLS_PAD
    row = lambda width: pl.BlockSpec((tm, width), lambda i: (i, 0))
    full = lambda a: pl.BlockSpec(a.shape, lambda i: (0,) * a.ndim)
    nw = attn_norm_w.reshape(1, d)
    out_shape = [jax.ShapeDtypeStruct((t, 1024), BF16),
                 jax.ShapeDtypeStruct((t, 1024), BF16),
                 jax.ShapeDtypeStruct((t, 1024), F32),
                 jax.ShapeDtypeStruct((t, 1024), BF16),
                 jax.ShapeDtypeStruct((t, 1024), BF16),
                 jax.ShapeDtypeStruct((t, MLA_Q_RANK), F32),
                 jax.ShapeDtypeStruct((t, MLA_KV_RANK), F32),
                 jax.ShapeDtypeStruct((t, LANES), F32),
                 jax.ShapeDtypeStruct((t, 1024), BF16),
                 jax.ShapeDtypeStruct((t, 1024), BF16)]
    return pl.pallas_call(
        functools.partial(_inproj_kernel, layer=layer),
        grid=(t // tm,),
        in_specs=[row(d), full(nw), full(lb_table),
                  pl.BlockSpec(w.shape, lambda i: (0, 0), pipeline_mode=pl.Buffered(1))],
        out_specs=[row(s.shape[1]) for s in out_shape],
        out_shape=out_shape,
        scratch_shapes=[pltpu.VMEM((tm, d), BF16)],
        compiler_params=_cparams(("parallel",)),
        name="in_proj",
    )(x2, nw, lb_table, w)


def _hgrn_tables():
    n = HG_CHUNK
    r = np.arange(n)[:, None]
    j = np.arange(n)[None, :]
    mats = [j <= r]
    for m in HG_LOW_BLOCKS:
        c = (r // (2 * m)) * (2 * m) + m - 1
        odd = (r & m) != 0
        mats.append(np.where(odd, (j > c) & (j <= r), (j > r) & (j <= c)))
    tab = np.concatenate(mats, axis=0).astype(np.float32)
    return jnp.asarray(np.concatenate([tab, tab], axis=1), dtype=BF16)


def _hgrn_kernel(q_ref, kk_ref, lf_ref, v_ref, g_ref, nw_ref, tab_ref, o_ref, st_ref, *, heads, n_chunks):
    n = HG_CHUNK
    unroll = math.gcd(n_chunks, HG_CHUNK_UNROLL)
    assert n == HG_DIM
    row = lax.broadcasted_iota(I32, (n, n), 0)
    col = lax.broadcasted_iota(I32, (n, n), 1)
    code = jnp.where(col <= row, 32 - lax.clz(row ^ col), -1)
    st_ref[...] = jnp.zeros_like(st_ref)
    nw = nw_ref[...]

    def load(r0, h):
        ls = slice(h * HG_DIM, (h + 1) * HG_DIM)
        qb = q_ref[pl.ds(r0, n), ls]
        kb = kk_ref[pl.ds(r0, n), ls]
        hi, lo = _split_bf16(lf_ref[pl.ds(r0, n), ls])
        ex = _dot(tab_ref[...], jnp.concatenate([hi, lo], axis=0))
        return dict(r0=r0, h=h, ls=ls, qb=qb, kb=kb, vb=v_ref[pl.ds(r0, n), ls], ex=ex)

    def low_products(p):
        q, k, ex = p["qb"].astype(F32), p["kb"].astype(F32), p["ex"]
        prods = [_dot_nt(p["qb"], p["kb"])]
        for i, m in enumerate(HG_LOW_BLOCKS):
            odd = (row & m) != 0
            w = jnp.where(odd, q, k) * jnp.exp(ex[(i + 1) * n:(i + 2) * n, :])
            prods.append(_dot_nt(jnp.where(odd, w, 0.0).astype(BF16), jnp.where(odd, 0.0, w).astype(BF16)))
        p["low"] = prods

    def high_products(p):
        q, k, b = p["qb"].astype(F32), p["kb"].astype(F32), p["ex"][0:n, :]
        prods = []
        for m in HG_HIGH_BLOCKS:
            bases = range(0, n, 2 * m)
            e_parts, s_parts = [], []
            for base in bases:
                mid = b[base + m - 1:base + m, :]
                e_parts += [mid - b[base:base + m, :], b[base + m:base + 2 * m, :] - mid]
                s_parts += [k[base:base + m, :], q[base + m:base + 2 * m, :]]
            w = jnp.concatenate(s_parts, axis=0) * jnp.exp(jnp.concatenate(e_parts, axis=0))
            zero = jnp.zeros((m, HG_DIM), F32)
            kp = jnp.concatenate([x for base in bases for x in (w[base:base + m, :], zero)], axis=0)
            qc = jnp.concatenate([w[base + m:base + 2 * m, :] for base in bases], axis=0)
            prods.append(_dot_nt(qc.astype(BF16), kp.astype(BF16)))
        p["high"] = prods

    def intra(p):
        a = jnp.where(code == 0, p["low"][0], 0.0)
        for m, r in zip(HG_LOW_BLOCKS, p["low"][1:]):
            a = jnp.where(code == m.bit_length(), r, a)
        for m, r in zip(HG_HIGH_BLOCKS, p["high"]):
            pieces = []
            for j, base in enumerate(range(0, n, 2 * m)):
                second = slice(base + m, base + 2 * m)
                pieces += [a[base:base + m, :],
                           jnp.where(code[second, :] == m.bit_length(), r[j * m:(j + 1) * m, :], a[second, :])]
            a = jnp.concatenate(pieces, axis=0)
        q, k, b = p["qb"].astype(F32), p["kb"].astype(F32), p["ex"][0:n, :]
        b_last = b[n - 1:n, :]
        p["o"] = _dot(a.astype(BF16), p["vb"])
        p["qe"] = (q * jnp.exp(b)).astype(BF16)
        p["upd"] = _dot_tn(p["vb"], (k * jnp.exp(b_last - b)).astype(BF16))
        p["decay"] = jnp.exp(b_last)

    def finish(p, st):
        o = p["o"] + _dot_nt(p["qe"], st.astype(BF16))
        ms = jnp.mean(o * o, axis=-1, keepdims=True)
        y = o * lax.rsqrt(ms + EPS) * nw
        o_ref[pl.ds(p["r0"], n), p["ls"]] = (y * g_ref[pl.ds(p["r0"], n), p["ls"]].astype(F32)).astype(BF16)
        return st * p["decay"] + p["upd"]

    stages = (low_products, high_products, intra)

    def chunk_group(cg, carry):
        pairs = [(u, h) for u in range(unroll) for h in range(heads)]
        state = [st_ref[h] for h in range(heads)]
        live = {}
        for step in range(len(pairs) + len(stages) + 1):
            if step < len(pairs):
                u, h = pairs[step]
                live[step] = load(pl.multiple_of((cg * unroll + u) * n, n), h)
            for d, stage in enumerate(stages, start=1):
                if 0 <= step - d < len(pairs):
                    stage(live[step - d])
            done = step - len(stages) - 1
            if 0 <= done < len(pairs):
                h = pairs[done][1]
                state[h] = finish(live.pop(done), state[h])
        for h in range(heads):
            st_ref[h] = state[h]
        return carry

    lax.fori_loop(0, n_chunks // unroll, chunk_group, 0)


def _hgrn(q, kk, lf, v, g, out_norm_w, batch, seq):
    t, width = q.shape
    hp = HG_HEADS_PER_STEP
    wblk = hp * HG_DIM
    tab = _hgrn_tables()
    spec = pl.BlockSpec((seq, wblk), lambda b, j: (b, j))
    full = lambda a: pl.BlockSpec(a.shape, lambda b, j: (0,) * a.ndim)
    nw = out_norm_w.reshape(1, HG_DIM)
    return pl.pallas_call(
        functools.partial(_hgrn_kernel, heads=hp, n_chunks=seq // HG_CHUNK),
        grid=(batch, width // wblk),
        in_specs=[spec, spec, spec, spec, spec, full(nw), full(tab)],
        out_specs=spec,
        out_shape=jax.ShapeDtypeStruct((t, width), BF16),
        scratch_shapes=[pltpu.VMEM((hp, HG_DIM, HG_DIM), F32)],
        compiler_params=_cparams(("parallel", "parallel")),
        name="hgrn",
    )(q, kk, lf, v, g, nw, tab)


def _rope(p, cos_t, sin_t):
    return p * cos_t + pltpu.roll(p, LANES // 2, axis=1) * sin_t


def _mla_prep_kernel(cq_ref, ckv_ref, kr_ref, pos_ref, qnw_ref, kvnw_ref, wq_ref, wkv_ref, rot_ref,
                     q_ref, k_ref, v_ref):
    ang = pos_ref[...] * rot_ref[0:1, :]
    cos_t = jnp.cos(ang) * rot_ref[1:2, :]
    sin_t = jnp.sin(ang) * rot_ref[2:3, :]

    cq = cq_ref[...]
    cqn = cq * lax.rsqrt(jnp.mean(cq * cq, axis=-1, keepdims=True) + EPS) * qnw_ref[...]
    ckv = ckv_ref[...]
    ckvn = ckv * lax.rsqrt(jnp.mean(ckv * ckv, axis=-1, keepdims=True) + EPS) * kvnw_ref[...]
    cqn = cqn.astype(BF16)
    ckvn = ckvn.astype(BF16)
    k_pe = _rope(kr_ref[...], cos_t, sin_t).astype(BF16)
    scale = MLA_QK_DIM ** -0.5
    for h in range(MLA_HEADS):
        c0 = h * MLA_QK_PAD
        qh = _dot(cqn, wq_ref[:, c0:c0 + MLA_QK_PAD])
        q_ref[h, :, 0:MLA_NOPE] = (qh[:, :MLA_NOPE] * scale).astype(BF16)
        q_ref[h, :, MLA_NOPE:MLA_QK_PAD] = (_rope(qh[:, MLA_NOPE:], cos_t, sin_t) * scale).astype(BF16)
        kvh = _dot(ckvn, wkv_ref[:, c0:c0 + MLA_NOPE + MLA_VDIM])
        k_ref[h, :, 0:MLA_NOPE] = kvh[:, :MLA_NOPE].astype(BF16)
        k_ref[h, :, MLA_NOPE:MLA_QK_PAD] = k_pe
        v_ref[h] = kvh[:, MLA_NOPE:].astype(BF16)


def _mla_prep(cq, ckv, kr, positions, q_norm_w, w_uq, kv_norm_w, w_ukv, batch, seq, tm):
    t = cq.shape[0]
    half = MLA_ROPE // 2
    wq = w_uq.reshape(MLA_Q_RANK, MLA_HEADS, MLA_QK_DIM)
    zq = jnp.zeros((MLA_Q_RANK, MLA_HEADS, half), w_uq.dtype)
    wq = jnp.concatenate([wq[:, :, :MLA_NOPE], wq[:, :, MLA_NOPE:MLA_NOPE + half], zq,
                          wq[:, :, MLA_NOPE + half:], zq], axis=2)
    wq = wq.reshape(MLA_Q_RANK, MLA_HEADS * MLA_QK_PAD).astype(BF16)
    wkv = w_ukv.astype(BF16)
    inv_freq = ROPE_THETA ** (-jnp.arange(half, dtype=F32) / half)
    z = jnp.zeros((half,), F32)
    o = jnp.ones((half,), F32)
    rot = jnp.stack([jnp.concatenate([inv_freq, z, inv_freq, z]),
                     jnp.concatenate([o, z, o, z]),
                     jnp.concatenate([-o, z, o, z])])
    pos = positions.astype(F32).reshape(t, 1)
    nblk = seq // tm
    row = lambda width: pl.BlockSpec((tm, width), lambda i: (i, 0))
    full = lambda a: pl.BlockSpec(a.shape, lambda i: (0,) * a.ndim)
    hspec = lambda width: pl.BlockSpec((None, MLA_HEADS, tm, width), lambda i: (i // nblk, 0, i % nblk, 0))
    qnw = q_norm_w.reshape(1, -1)
    kvnw = kv_norm_w.reshape(1, -1)
    return pl.pallas_call(
        _mla_prep_kernel,
        grid=(t // tm,),
        in_specs=[row(MLA_Q_RANK), row(MLA_KV_RANK), row(LANES), row(1), full(qnw), full(kvnw),
                  full(wq), full(wkv), full(rot)],
        out_specs=[hspec(MLA_QK_PAD), hspec(MLA_QK_PAD), hspec(MLA_VDIM)],
        out_shape=[jax.ShapeDtypeStruct((batch, MLA_HEADS, seq, MLA_QK_PAD), BF16),
                   jax.ShapeDtypeStruct((batch, MLA_HEADS, seq, MLA_QK_PAD), BF16),
                   jax.ShapeDtypeStruct((batch, MLA_HEADS, seq, MLA_VDIM), BF16)],
        compiler_params=_cparams(("parallel",)),
        name="mla_prep",
    )(cq, ckv, kr, pos, qnw, kvnw, wq, wkv, rot)


def _flash_kernel(q_ref, k_ref, v_ref, o_ref, v1_scr, *, tq):
    seq = q_ref.shape[0]
    nq = seq // tq
    row = lax.broadcasted_iota(I32, (tq, tq), 0)
    col = lax.broadcasted_iota(I32, (tq, tq), 1)
    causal = row >= col

    def scores(i):
        q = q_ref[i * tq:(i + 1) * tq, :]
        return _dot_nt(q, k_ref[0:(i + 1) * tq, :])

    def softmax(i, s):
        lo = i * tq
        sd = jnp.where(causal, s[:, lo:lo + tq], -jnp.inf)
        m = jnp.max(sd, axis=-1, keepdims=True)
        if i > 0:
            m = jnp.maximum(m, jnp.max(s[:, 0:lo], axis=-1, keepdims=True))
        p = jnp.exp(jnp.concatenate([s[:, 0:lo], sd], axis=1) - m) if i > 0 else jnp.exp(sd - m)
        return p.astype(BF16)

    v1_scr[:, 0:MLA_VDIM] = v_ref[...]
    v1_scr[:, MLA_VDIM:] = jnp.ones((seq, MLA_VDIM), BF16)

    def output(i, p):
        hi = (i + 1) * tq
        acc = _dot(p, v1_scr[0:hi, :])
        o_ref[i * tq:hi, :] = (acc[:, 0:MLA_VDIM] / acc[:, MLA_VDIM:]).astype(BF16)

    s = {0: scores(0)}
    if nq > 1:
        s[1] = scores(1)
    pending = softmax(0, s.pop(0))
    for i in range(nq):
        if i + 2 < nq:
            s[i + 2] = scores(i + 2)
        output(i, pending)
        if i + 1 < nq:
            pending = softmax(i + 1, s.pop(i + 1))


def _flash(q, k, v, tq):
    batch, heads, seq, _ = q.shape
    qk = pl.BlockSpec((None, None, seq, MLA_QK_PAD), lambda b, h: (b, h, 0, 0))
    return pl.pallas_call(
        functools.partial(_flash_kernel, tq=tq),
        grid=(batch, heads),
        in_specs=[qk, qk, pl.BlockSpec((None, None, seq, MLA_VDIM), lambda b, h: (b, h, 0, 0))],
        out_specs=pl.BlockSpec((None, seq, MLA_VDIM), lambda b, h: (b, 0, h)),
        out_shape=jax.ShapeDtypeStruct((batch, seq, heads * MLA_VDIM), BF16),
        scratch_shapes=[pltpu.VMEM((seq, 2 * MLA_VDIM), BF16)],
        compiler_params=_cparams(("parallel", "parallel")),
        name="flash",
    )(q, k, v)


_R_E0, _R_E1, _R_W0, _R_W1, _R_RANK0, _R_RANK1 = 0, 1, 2, 3, 4, 5


def _first_lane_equal(x, value, lane):
    return jnp.min(jnp.where(x == value, lane, LANES), axis=-1, keepdims=True)


def _combine_kernel(oa_ref, ob_ref, sa_ref, sb_ref, x_ref, wa_ref, wb_ref, wo_ref, fnw_ref, wr_ref, br_ref,
                    lt_ref, x1_ref, h2_ref, rec_ref, cnt_ref, run_ref, y_scr):
    i = pl.program_id(0)

    @pl.when(i == 0)
    def _():
        run_ref[...] = jnp.zeros_like(run_ref)

    sub = lt_ref.shape[0]
    n_sub = x_ref.shape[0] // sub
    half = x_ref.shape[1] // 2
    rows = lambda s: slice(s * sub, (s + 1) * sub)

    def branches(s):
        for c in (0, half):
            cols = slice(c, c + half)
            ya = _dot(oa_ref[rows(s), :], wa_ref[:, cols])
            yb = _dot(ob_ref[rows(s), :], wb_ref[:, cols])
            y = sa_ref[rows(s), cols].astype(F32) * ya + sb_ref[rows(s), cols].astype(F32) * yb
            y_scr[rows(s), cols] = y.astype(BF16)

    def residual(s):
        for c in (0, half):
            cols = slice(c, c + half)
            x1_ref[rows(s), cols] = x_ref[rows(s), cols] + _dot(y_scr[rows(s), :], wo_ref[:, cols])

    def router(s):
        x1 = x1_ref[rows(s), :]
        h2 = x1 * lax.rsqrt(jnp.mean(x1 * x1, axis=-1, keepdims=True) + EPS) * fnw_ref[...]
        h2_ref[rows(s), :] = h2
        hh, hl = _split_bf16(h2)
        return _dot(hh, wr_ref[0]) + _dot(hl, wr_ref[0]) + _dot(hh, wr_ref[1]) + br_ref[...]

    run = run_ref[...]
    logits = {}
    for step in range(n_sub + 3):
        if step < n_sub:
            branches(step)
        if 0 <= step - 1 < n_sub:
            residual(step - 1)
        if 0 <= step - 2 < n_sub:
            logits[step - 2] = router(step - 2)
        if 0 <= step - 3 < n_sub:
            rec, run = _route(logits.pop(step - 3), run, lt_ref[...])
            rec_ref[rows(step - 3), :] = rec
    run_ref[...] = run
    cnt_ref[...] = run


def _route(logits, run, lt):
    tm = logits.shape[0]
    lane = lax.broadcasted_iota(I32, (tm, LANES), 1)
    neg = -jnp.inf

    gl = jnp.where(lane < N_GROUPS, logits, neg)
    ge = jnp.exp(gl - jnp.max(gl, axis=-1, keepdims=True))
    gp = ge / jnp.sum(ge, axis=-1, keepdims=True)
    g_w = jnp.max(gp, axis=-1, keepdims=True)
    g_idx = _first_lane_equal(gp, g_w, lane)

    ej = lane - EXPERT_LANE0
    sel = (ej >= 0) & (ej < N_EXPERTS) & ((ej >> 3) == g_idx)
    el = jnp.where(sel, logits, neg)
    ee = jnp.exp(el - jnp.max(el, axis=-1, keepdims=True))
    ep = jnp.where(sel, ee / jnp.sum(ee, axis=-1, keepdims=True), -1.0)
    p0 = jnp.max(ep, axis=-1, keepdims=True)
    l0 = _first_lane_equal(ep, p0, lane)
    ep1 = jnp.where(lane == l0, -1.0, ep)
    p1 = jnp.max(ep1, axis=-1, keepdims=True)
    l1 = _first_lane_equal(ep1, p1, lane)
    psum = p0 + p1
    w0 = g_w * (p0 / psum)
    w1 = g_w * (p1 / psum)

    oh0 = lane == l0
    oh1 = lane == l1
    oh = jnp.where(oh0 | oh1, 1.0, 0.0)
    before = run + _dot(lt, oh.astype(BF16))
    rank0 = jnp.sum(jnp.where(oh0, before, 0.0), axis=-1, keepdims=True)
    rank1 = jnp.sum(jnp.where(oh1, before, 0.0), axis=-1, keepdims=True)
    run = run + jnp.sum(oh, axis=0, keepdims=True)

    rec = jnp.where(lane == _R_E0, (l0 - EXPERT_LANE0).astype(F32), 0.0)
    rec = jnp.where(lane == _R_E1, (l1 - EXPERT_LANE0).astype(F32), rec)
    rec = jnp.where(lane == _R_W0, w0, rec)
    rec = jnp.where(lane == _R_W1, w1, rec)
    rec = jnp.where(lane == _R_RANK0, rank0, rec)
    rec = jnp.where(lane == _R_RANK1, rank1, rec)
    return rec, run


def _combine(o_a, o_b, sa, sb, x2, w_a, w_b, w_o, ffn_norm_w, w_group, b_group, w_expert, b_expert, tm):
    t, d = x2.shape
    pad = LANES - N_GROUPS - N_EXPERTS
    wr = jnp.concatenate([w_group, w_expert, jnp.zeros((d, pad), F32)], axis=1)
    wr_hi = wr.astype(BF16)
    wr_lo = (wr - wr_hi.astype(F32)).astype(BF16)
    wr2 = jnp.stack([wr_hi, wr_lo])
    br = jnp.concatenate([b_group, b_expert, jnp.zeros((pad,), F32)]).reshape(1, LANES)
    lt = jnp.asarray(np.tril(np.ones((COMBINE_SUB, COMBINE_SUB), np.float32), -1), dtype=BF16)
    fnw = ffn_norm_w.reshape(1, d)
    row = lambda width: pl.BlockSpec((tm, width), lambda i: (i, 0))
    full = lambda a: pl.BlockSpec(a.shape, lambda i: (0,) * a.ndim)
    wa, wb, wo = w_a.astype(BF16), w_b.astype(BF16), w_o.astype(BF16)
    return pl.pallas_call(
        _combine_kernel,
        grid=(t // tm,),
        in_specs=[row(d), row(d), row(d), row(d), row(d), full(wa), full(wb), full(wo), full(fnw),
                  full(wr2), full(br), full(lt)],
        out_specs=[row(d), row(d), row(LANES), pl.BlockSpec((1, LANES), lambda i: (0, 0))],
        out_shape=[jax.ShapeDtypeStruct((t, d), F32),
                   jax.ShapeDtypeStruct((t, d), F32),
                   jax.ShapeDtypeStruct((t, LANES), F32),
                   jax.ShapeDtypeStruct((1, LANES), F32)],
        scratch_shapes=[pltpu.VMEM((1, LANES), F32), pltpu.VMEM((tm, d), BF16)],
        compiler_params=_cparams(("arbitrary",)),
        name="combine",
    )(o_a, o_b, sa, sb, x2, wa, wb, wo, fnw, wr2, br, lt)


def _row_copy(src_ref, src_row, dst_ref, dst_row, sem):
    return pltpu.make_async_copy(src_ref.at[pl.ds(src_row, 1)], dst_ref.at[pl.ds(dst_row, 1)], sem)


def _dest_kernel(rec_ref, ps_ref, d_ref):
    rec = rec_ref[...]
    lane = lax.broadcasted_iota(I32, rec.shape, 1)
    ps = ps_ref[...]

    def lookup(e_lane, r_lane):
        e = rec[:, e_lane:e_lane + 1].astype(I32) + EXPERT_LANE0
        start = jnp.sum(jnp.where(lane == e, ps, 0.0), axis=-1, keepdims=True)
        return (start + rec[:, r_lane:r_lane + 1]).astype(I32)

    d0 = lookup(_R_E0, _R_RANK0)
    d1 = lookup(_R_E1, _R_RANK1)
    d_ref[...] = jnp.where(lane == 0, d0, jnp.where(lane == 1, d1, 0))


def _dest(rec, pstart, tm):
    t = rec.shape[0]
    ps = jnp.zeros((1, LANES), F32).at[0, EXPERT_LANE0:EXPERT_LANE0 + N_EXPERTS].set(pstart.astype(F32))
    return pl.pallas_call(
        _dest_kernel,
        grid=(t // tm,),
        in_specs=[pl.BlockSpec((tm, LANES), lambda i: (i, 0)), pl.BlockSpec((1, LANES), lambda i: (0, 0))],
        out_specs=pl.BlockSpec((tm, LANES), lambda i: (i, 0)),
        out_shape=jax.ShapeDtypeStruct((t, LANES), I32),
        compiler_params=_cparams(("parallel",)),
        name="dest",
    )(rec, ps)


def _dispatch_kernel(d_ref, h_ref, xs_in_ref, xs_ref, sem):
    del xs_in_ref
    tm = h_ref.shape[0]

    def start(j, carry):
        for u in range(DMA_UNROLL):
            tok = j * DMA_UNROLL + u
            for k in range(2):
                _row_copy(h_ref, tok, xs_ref, d_ref[2 * tok + k], sem).start(priority=k)
        return carry

    lax.fori_loop(0, tm // DMA_UNROLL, start, 0)

    def wait(j, carry):
        for _ in range(2 * DMA_UNROLL):
            _row_copy(h_ref, 0, xs_ref, 0, sem).wait()
        return carry

    lax.fori_loop(0, tm // DMA_UNROLL, wait, 0)


def _dispatch(h2, dest, cap, tm):
    t, d = h2.shape
    xs0 = jnp.zeros((cap, d), h2.dtype)
    return pl.pallas_call(
        _dispatch_kernel,
        grid=(t // tm,),
        in_specs=[pl.BlockSpec((2 * tm,), lambda i: (i,), memory_space=pltpu.SMEM),
                  pl.BlockSpec((tm, d), lambda i: (i, 0)),
                  pl.BlockSpec(memory_space=pl.ANY)],
        out_specs=pl.BlockSpec(memory_space=pl.ANY),
        out_shape=jax.ShapeDtypeStruct((cap, d), h2.dtype),
        scratch_shapes=[pltpu.SemaphoreType.DMA(())],
        input_output_aliases={2: 0},
        compiler_params=_cparams(("arbitrary",)),
        name="dispatch",
    )(dest, h2, xs0)


def _experts_kernel(be_ref, nu_ref, xs_ref, w1_ref, w3_ref, w2_ref, ys_ref, w1b, w3b, w2b):
    i = pl.program_id(0)

    @pl.when(i < nu_ref[0])
    def _():
        changed = jnp.logical_or(i == 0, be_ref[i] != be_ref[jnp.maximum(i - 1, 0)])

        @pl.when(changed)
        def _():
            w1b[...] = w1_ref[...].astype(BF16)
            w3b[...] = w3_ref[...].astype(BF16)
            w2b[...] = w2_ref[...].astype(BF16)

        grp = xs_ref.shape[0] // MOE_ROW_GROUPS
        rows = lambda g: slice(g * grp, (g + 1) * grp)

        def up(g):
            x = xs_ref[rows(g), :].astype(BF16)
            return _dot(x, w1b[...]), _dot(x, w3b[...])

        def down(g, h1, h3):
            hid = h1 * jax.nn.sigmoid(h1) * h3
            ys_ref[rows(g), :] = _dot(hid.astype(BF16), w2b[...])

        nxt = up(0)
        for g in range(MOE_ROW_GROUPS):
            cur = nxt
            if g + 1 < MOE_ROW_GROUPS:
                nxt = up(g + 1)
            down(g, *cur)

    @pl.when(i >= nu_ref[0])
    def _():
        ys_ref[...] = jnp.zeros_like(ys_ref)


def _experts(xs, w1, w3, w2, block_expert, n_used):
    cap, d = xs.shape
    bm = MOE_ROWS
    ff = w1.shape[2]
    rows = pl.BlockSpec((bm, d), lambda i, be, nu: (jnp.minimum(i, nu[0] - 1), 0))
    return pl.pallas_call(
        _experts_kernel,
        grid_spec=pltpu.PrefetchScalarGridSpec(
            num_scalar_prefetch=2, grid=(cap // bm,),
            in_specs=[rows,
                      pl.BlockSpec((None, d, ff), lambda i, be, nu: (be[i], 0, 0)),
                      pl.BlockSpec((None, d, ff), lambda i, be, nu: (be[i], 0, 0)),
                      pl.BlockSpec((None, ff, d), lambda i, be, nu: (be[i], 0, 0))],
            out_specs=pl.BlockSpec((bm, d), lambda i, be, nu: (i, 0)),
            scratch_shapes=[pltpu.VMEM((d, ff), BF16), pltpu.VMEM((d, ff), BF16), pltpu.VMEM((ff, d), BF16)]),
        out_shape=jax.ShapeDtypeStruct((cap, d), F32),
        compiler_params=_cparams(("arbitrary",)),
        name="experts",
    )(block_expert, n_used, xs, w1, w3, w2)


def _final_kernel(d_ref, x1_ref, rec_ref, nw_ref, ys_ref, o_ref, buf, sem):
    tm = x1_ref.shape[0]

    def start(j, carry):
        for u in range(DMA_UNROLL):
            tok = j * DMA_UNROLL + u
            for k in range(2):
                _row_copy(ys_ref, d_ref[2 * tok + k], buf.at[k], tok, sem).start(priority=k)
        return carry

    lax.fori_loop(0, tm // DMA_UNROLL, start, 0)

    def wait(j, carry):
        for _ in range(2 * DMA_UNROLL):
            _row_copy(ys_ref, 0, buf.at[0], 0, sem).wait()
        return carry

    lax.fori_loop(0, tm // DMA_UNROLL, wait, 0)

    rec = rec_ref[...]
    x = x1_ref[...] + buf[0] * rec[:, _R_W0:_R_W0 + 1] + buf[1] * rec[:, _R_W1:_R_W1 + 1]
    o_ref[...] = x * lax.rsqrt(jnp.mean(x * x, axis=-1, keepdims=True) + EPS) * nw_ref[...]


def _final(x1, rec, ys, dest, final_norm_w, tm):
    t, d = x1.shape
    nw = final_norm_w.reshape(1, d)
    return pl.pallas_call(
        _final_kernel,
        grid=(t // tm,),
        in_specs=[pl.BlockSpec((2 * tm,), lambda i: (i,), memory_space=pltpu.SMEM),
                  pl.BlockSpec((tm, d), lambda i: (i, 0)),
                  pl.BlockSpec((tm, LANES), lambda i: (i, 0)),
                  pl.BlockSpec((1, d), lambda i: (0, 0)),
                  pl.BlockSpec(memory_space=pl.ANY)],
        out_specs=pl.BlockSpec((tm, d), lambda i: (i, 0)),
        out_shape=jax.ShapeDtypeStruct((t, d), F32),
        scratch_shapes=[pltpu.VMEM((2, tm, d), F32), pltpu.SemaphoreType.DMA(())],
        compiler_params=_cparams(("arbitrary",)),
        name="final",
    )(dest, x1, rec, nw, ys)


def _moe_layout(counts):
    bm = MOE_ROWS
    padded = (counts + bm - 1) // bm * bm
    pends = jnp.cumsum(padded)
    pstart = pends - padded
    return pstart.astype(I32), pends.astype(I32)


def kernel(x, positions, attn_norm_w, w_in, hg_lower_bound, hg_out_norm_w, mla_q_norm_w, mla_w_uq, mla_kv_norm_w, mla_w_ukv, w_branch_hgrn, w_branch_mla, w_out, ffn_norm_w, router_group_w, router_group_b, router_expert_w, router_expert_b, expert_w1, expert_w3, expert_w2, final_norm_w):
    batch, seq, d = x.shape
    t = batch * seq
    depth = w_in.shape[0]
    assert d == 1024 and seq % 512 == 0 and t % 1024 == 0
    x2 = x.reshape(t, d)
    for l in range(depth):
        q, kk, lf, v, g, cq, ckv, kr, sa, sb = _in_proj(x2, attn_norm_w[l], w_in[l], hg_lower_bound, l, tm=256)
        o_a = _hgrn(q, kk, lf, v, g, hg_out_norm_w[l], batch, seq)
        mq, mk, mv = _mla_prep(cq, ckv, kr, positions, mla_q_norm_w[l], mla_w_uq[l], mla_kv_norm_w[l],
                               mla_w_ukv[l], batch, seq, tm=512)
        o_b = _flash(mq, mk, mv, tq=256).reshape(t, MLA_HEADS * MLA_VDIM)
        x1, h2, rec, cnt = _combine(o_a, o_b, sa, sb, x2, w_branch_hgrn[l], w_branch_mla[l], w_out[l],
                                    ffn_norm_w[l], router_group_w[l], router_group_b[l],
                                    router_expert_w[l], router_expert_b[l], tm=1024)

        counts = cnt[0, EXPERT_LANE0:EXPERT_LANE0 + N_EXPERTS].astype(I32)
        pstart, pends = _moe_layout(counts)
        n_blocks = (2 * t) // MOE_ROWS + N_EXPERTS
        cap = n_blocks * MOE_ROWS
        n_used = (pends[-1] // MOE_ROWS).reshape(1)
        blk = jnp.minimum(jnp.arange(n_blocks, dtype=I32), n_used[0] - 1)
        be = jnp.sum((blk[:, None] * MOE_ROWS >= pends[None, :]).astype(I32), axis=1)
        dest = _dest(rec, pstart, tm=512)[:, 0:2].reshape(2 * t)

        xs = _dispatch(h2, dest, cap, tm=512)
        ys = _experts(xs, expert_w1[l], expert_w3[l], expert_w2[l], be, n_used)
        assert l == depth - 1, "multi-layer stacking needs an un-normalised residual output"
        out = _final(x1, rec, ys, dest, final_norm_w, tm=512)
    return out.reshape(batch, seq, d)
```

```python
import functools
import math

import numpy as np
import jax
import jax.numpy as jnp
from jax import lax
from jax.experimental import pallas as pl
from jax.experimental.pallas import tpu as pltpu

F32 = jnp.float32
BF16 = jnp.bfloat16
I32 = jnp.int32

EPS = 1e-6
LANES = 128
HG_HEADS = 8
HG_DIM = 128
MLA_HEADS = 8
MLA_Q_RANK = 384
MLA_KV_RANK = 256
MLA_NOPE = 128
MLA_ROPE = 64
MLA_VDIM = 128
MLA_QK_DIM = MLA_NOPE + MLA_ROPE
MLA_QK_PAD = 256
ROPE_THETA = 10000.0
N_GROUPS = 8
EXPERTS_PER_GROUP = 8
N_EXPERTS = N_GROUPS * EXPERTS_PER_GROUP
EXPERT_LANE0 = N_GROUPS
EXPERT_FF = 512

HG_CHUNK = 128
HG_LOW_BLOCKS = (1, 2, 4)
HG_HIGH_BLOCKS = (8, 16, 32, 64)
HG_HEADS_PER_STEP = 2
HG_CHUNK_UNROLL = 8
COMBINE_SUB = 256
MOE_ROWS = 256
MOE_ROW_GROUPS = 2
DMA_UNROLL = 8
VMEM_LIMIT = 56 * 1024 * 1024


def _cparams(sem):
    return pltpu.CompilerParams(dimension_semantics=sem, vmem_limit_bytes=VMEM_LIMIT)


def _dot(a, b):
    return jnp.dot(a, b, preferred_element_type=F32)


def _dot_nt(a, b):
    return lax.dot_general(a, b, (((1,), (1,)), ((), ())), preferred_element_type=F32)


def _dot_tn(a, b):
    return lax.dot_general(a, b, (((0,), (0,)), ((), ())), preferred_element_type=F32)


def _split_bf16(x):
    hi = x.astype(BF16)
    lo = (x - hi.astype(F32)).astype(BF16)
    return hi, lo


_IN_OFF_Q = 0
_IN_OFF_F = 1024
_IN_OFF_I = 2048
_IN_OFF_G = 3072
_IN_OFF_CQ = 4096
_IN_OFF_CKV = _IN_OFF_CQ + MLA_Q_RANK
_IN_OFF_KR = _IN_OFF_CKV + MLA_KV_RANK
_IN_OFF_GA = _IN_OFF_KR + LANES
_IN_OFF_GB = _IN_OFF_GA + 1024
_IN_COLS_PAD = _IN_OFF_GB + 1024


def _rope_lane_layout(w):
    half = MLA_ROPE // 2
    z = jnp.zeros((w.shape[0], half), w.dtype)
    return jnp.concatenate([w[:, :half], z, w[:, half:], z], axis=1)


def _inproj_kernel(x_ref, nw_ref, lbt_ref, w_ref, q_ref, kk_ref, lf_ref, v_ref, g_ref, cq_ref, ckv_ref,
                   kr_ref, sa_ref, sb_ref, h_scr, *, layer):
    x = x_ref[...]
    ms = jnp.mean(x * x, axis=-1, keepdims=True)
    h_scr[...] = (x * lax.rsqrt(ms + EPS) * nw_ref[...]).astype(BF16)

    tab = lbt_ref[...]
    e = jnp.exp(tab - jnp.max(tab, axis=0, keepdims=True))
    sm = e / jnp.sum(e, axis=0, keepdims=True)
    lb = jnp.sum(sm[:layer + 1], axis=0, keepdims=True)

    def proj(off, width):
        return _dot(h_scr[...], w_ref[:, off:off + width])

    half = 512
    for c in range(0, 1024, half):
        z = proj(_IN_OFF_Q + c, half)
        q_ref[:, c:c + half] = (z * jax.nn.sigmoid(z) * (HG_DIM ** -0.5)).astype(BF16)
        z = proj(_IN_OFF_F + c, half)
        lbc = lb[:, c:c + half]
        fg = lbc + (1.0 - lbc) * jax.nn.sigmoid(z)
        lf_ref[:, c:c + half] = jnp.log(fg)
        kk_ref[:, c:c + half] = (1.0 - fg).astype(BF16)
        v_ref[:, c:c + half] = proj(_IN_OFF_I + c, half).astype(BF16)
        z = proj(_IN_OFF_G + c, half)
        g_ref[:, c:c + half] = (z * jax.nn.sigmoid(z)).astype(BF16)
        sa_ref[:, c:c + half] = jax.nn.sigmoid(proj(_IN_OFF_GA + c, half)).astype(BF16)
        sb_ref[:, c:c + half] = jax.nn.sigmoid(proj(_IN_OFF_GB + c, half)).astype(BF16)
    cq_ref[...] = proj(_IN_OFF_CQ, MLA_Q_RANK)
    ckv_ref[...] = proj(_IN_OFF_CKV, MLA_KV_RANK)
    kr_ref[...] = proj(_IN_OFF_KR, LANES)


def _in_proj(x2, attn_norm_w, w_in, lb_table, layer, tm):
    t, d = x2.shape
    parts = []
    off = 0
    for width in (1024, 1024, 1024, 1024, MLA_Q_RANK, MLA_KV_RANK, MLA_ROPE, 1024, 1024):
        parts.append(w_in[:, off:off + width])
        off += width
    parts[6] = _rope_lane_layout(parts[6])
    w = jnp.concatenate(parts, axis=1).astype(BF16)
    assert w.shape[1] == _IN_COLS_PAD
    row = lambda width: pl.BlockSpec((tm, width), lambda i: (i, 0))
    full = lambda a: pl.BlockSpec(a.shape, lambda i: (0,) * a.ndim)
    nw = attn_norm_w.reshape(1, d)
    out_shape = [jax.ShapeDtypeStruct((t, 1024), BF16),
                 jax.ShapeDtypeStruct((t, 1024), BF16),
                 jax.ShapeDtypeStruct((t, 1024), F32),
                 jax.ShapeDtypeStruct((t, 1024), BF16),
                 jax.ShapeDtypeStruct((t, 1024), BF16),
                 jax.ShapeDtypeStruct((t, MLA_Q_RANK), F32),
                 jax.ShapeDtypeStruct((t, MLA_KV_RANK), F32),
                 jax.ShapeDtypeStruct((t, LANES), F32),
                 jax.ShapeDtypeStruct((t, 1024), BF16),
                 jax.ShapeDtypeStruct((t, 1024), BF16)]
    return pl.pallas_call(
        functools.partial(_inproj_kernel, layer=layer),
        grid=(t // tm,),
        in_specs=[row(d), full(nw), full(lb_table),
                  pl.BlockSpec(w.shape, lambda i: (0, 0), pipeline_mode=pl.Buffered(1))],
        out_specs=[row(s.shape[1]) for s in out_shape],
        out_shape=out_shape,
        scratch_shapes=[pltpu.VMEM((tm, d), BF16)],
        compiler_params=_cparams(("parallel",)),
        name="in_proj",
    )(x2, nw, lb_table, w)


def _hgrn_tables():
    n = HG_CHUNK
    r = np.arange(n)[:, None]
    j = np.arange(n)[None, :]
    mats = [j <= r]
    for m in HG_LOW_BLOCKS:
        c = (r // (2 * m)) * (2 * m) + m - 1
        odd = (r & m) != 0
        mats.append(np.where(odd, (j > c) & (j <= r), (j > r) & (j <= c)))
    tab = np.concatenate(mats, axis=0).astype(np.float32)
    return jnp.asarray(np.concatenate([tab, tab], axis=1), dtype=BF16)


def _hgrn_kernel(q_ref, kk_ref, lf_ref, v_ref, g_ref, nw_ref, tab_ref, o_ref, st_ref, *, heads, n_chunks):
    n = HG_CHUNK
    unroll = math.gcd(n_chunks, HG_CHUNK_UNROLL)
    assert n == HG_DIM
    row = lax.broadcasted_iota(I32, (n, n), 0)
    col = lax.broadcasted_iota(I32, (n, n), 1)
    code = jnp.where(col <= row, 32 - lax.clz(row ^ col), -1)
    st_ref[...] = jnp.zeros_like(st_ref)
    nw = nw_ref[...]

    def load(r0, h):
        ls = slice(h * HG_DIM, (h + 1) * HG_DIM)
        qb = q_ref[pl.ds(r0, n), ls]
        kb = kk_ref[pl.ds(r0, n), ls]
        hi, lo = _split_bf16(lf_ref[pl.ds(r0, n), ls])
        ex = _dot(tab_ref[...], jnp.concatenate([hi, lo], axis=0))
        return dict(r0=r0, h=h, ls=ls, qb=qb, kb=kb, vb=v_ref[pl.ds(r0, n), ls], ex=ex)

    def low_products(p):
        q, k, ex = p["qb"].astype(F32), p["kb"].astype(F32), p["ex"]
        prods = [_dot_nt(p["qb"], p["kb"])]
        for i, m in enumerate(HG_LOW_BLOCKS):
            odd = (row & m) != 0
            w = jnp.where(odd, q, k) * jnp.exp(ex[(i + 1) * n:(i + 2) * n, :])
            prods.append(_dot_nt(jnp.where(odd, w, 0.0).astype(BF16), jnp.where(odd, 0.0, w).astype(BF16)))
        p["low"] = prods

    def high_products(p):
        q, k, b = p["qb"].astype(F32), p["kb"].astype(F32), p["ex"][0:n, :]
        prods = []
        for m in HG_HIGH_BLOCKS:
            bases = range(0, n, 2 * m)
            e_parts, s_parts = [], []
            for base in bases:
                mid = b[base + m - 1:base + m, :]
                e_parts += [mid - b[base:base + m, :], b[base + m:base + 2 * m, :] - mid]
                s_parts += [k[base:base + m, :], q[base + m:base + 2 * m, :]]
            w = jnp.concatenate(s_parts, axis=0) * jnp.exp(jnp.concatenate(e_parts, axis=0))
            zero = jnp.zeros((m, HG_DIM), F32)
            kp = jnp.concatenate([x for base in bases for x in (w[base:base + m, :], zero)], axis=0)
            qc = jnp.concatenate([w[base + m:base + 2 * m, :] for base in bases], axis=0)
            prods.append(_dot_nt(qc.astype(BF16), kp.astype(BF16)))
        p["high"] = prods

    def intra(p):
        a = jnp.where(code == 0, p["low"][0], 0.0)
        for m, r in zip(HG_LOW_BLOCKS, p["low"][1:]):
            a = jnp.where(code == m.bit_length(), r, a)
        for m, r in zip(HG_HIGH_BLOCKS, p["high"]):
            pieces = []
            for j, base in enumerate(range(0, n, 2 * m)):
                second = slice(base + m, base + 2 * m)
                pieces += [a[base:base + m, :],
                           jnp.where(code[second, :] == m.bit_length(), r[j * m:(j + 1) * m, :], a[second, :])]
            a = jnp.concatenate(pieces, axis=0)
        q, k, b = p["qb"].astype(F32), p["kb"].astype(F32), p["ex"][0:n, :]
        b_last = b[n - 1:n, :]
        p["o"] = _dot(a.astype(BF16), p["vb"])
        p["qe"] = (q * jnp.exp(b)).astype(BF16)
        p["upd"] = _dot_tn(p["vb"], (k * jnp.exp(b_last - b)).astype(BF16))
        p["decay"] = jnp.exp(b_last)

    def finish(p, st):
        o = p["o"] + _dot_nt(p["qe"], st.astype(BF16))
        ms = jnp.mean(o * o, axis=-1, keepdims=True)
        y = o * lax.rsqrt(ms + EPS) * nw
        o_ref[pl.ds(p["r0"], n), p["ls"]] = (y * g_ref[pl.ds(p["r0"], n), p["ls"]].astype(F32)).astype(BF16)
        return st * p["decay"] + p["upd"]

    stages = (low_products, high_products, intra)

    def chunk_group(cg, carry):
        pairs = [(u, h) for u in range(unroll) for h in range(heads)]
        state = [st_ref[h] for h in range(heads)]
        live = {}
        for step in range(len(pairs) + len(stages) + 1):
            if step < len(pairs):
                u, h = pairs[step]
                live[step] = load(pl.multiple_of((cg * unroll + u) * n, n), h)
            for d, stage in enumerate(stages, start=1):
                if 0 <= step - d < len(pairs):
                    stage(live[step - d])
            done = step - len(stages) - 1
            if 0 <= done < len(pairs):
                h = pairs[done][1]
                state[h] = finish(live.pop(done), state[h])
        for h in range(heads):
            st_ref[h] = state[h]
        return carry

    lax.fori_loop(0, n_chunks // unroll, chunk_group, 0)


def _hgrn(q, kk, lf, v, g, out_norm_w, batch, seq):
    t, width = q.shape
    hp = HG_HEADS_PER_STEP
    wblk = hp * HG_DIM
    tab = _hgrn_tables()
    spec = pl.BlockSpec((seq, wblk), lambda b, j: (b, j))
    full = lambda a: pl.BlockSpec(a.shape, lambda b, j: (0,) * a.ndim)
    nw = out_norm_w.reshape(1, HG_DIM)
    return pl.pallas_call(
        functools.partial(_hgrn_kernel, heads=hp, n_chunks=seq // HG_CHUNK),
        grid=(batch, width // wblk),
        in_specs=[spec, spec, spec, spec, spec, full(nw), full(tab)],
        out_specs=spec,
        out_shape=jax.ShapeDtypeStruct((t, width), BF16),
        scratch_shapes=[pltpu.VMEM((hp, HG_DIM, HG_DIM), F32)],
        compiler_params=_cparams(("parallel", "parallel")),
        name="hgrn",
    )(q, kk, lf, v, g, nw, tab)


def _rope(p, cos_t, sin_t):
    return p * cos_t + pltpu.roll(p, LANES // 2, axis=1) * sin_t


def _rope_tables(pos, freq):
    rows = pos.shape[0]
    qr = rows // 4
    half = MLA_ROPE // 2
    lane = lax.broadcasted_iota(I32, (qr, LANES), 1)
    packed = jnp.where(lane < half, pos[0:qr], jnp.where(lane < 2 * half, pos[qr:2 * qr],
                       jnp.where(lane < 3 * half, pos[2 * qr:3 * qr], pos[3 * qr:])))
    ang = packed * freq
    cos_p, sin_p = jnp.cos(ang), jnp.sin(ang)
    first = lane < half
    second = (lane >= 2 * half) & (lane < 3 * half)
    roll = lambda x, shift: x if shift % LANES == 0 else pltpu.roll(x, shift % LANES, axis=1)
    cos_parts, sin_parts = [], []
    for j in range(4):
        to_first, to_second = -half * j, 2 * half - half * j
        cos_parts.append(jnp.where(first, roll(cos_p, to_first), jnp.where(second, roll(cos_p, to_second), 0.0)))
        sin_parts.append(jnp.where(first, -roll(sin_p, to_first), jnp.where(second, roll(sin_p, to_second), 0.0)))
    return jnp.concatenate(cos_parts, axis=0), jnp.concatenate(sin_parts, axis=0)


def _mla_prep_kernel(cq_ref, ckv_ref, kr_ref, pos_ref, qnw_ref, kvnw_ref, wq_ref, wkv_ref, rot_ref,
                     q_ref, k_ref, v_ref):
    cos_t, sin_t = _rope_tables(pos_ref[...], rot_ref[...])

    cq = cq_ref[...]
    cqn = cq * lax.rsqrt(jnp.mean(cq * cq, axis=-1, keepdims=True) + EPS) * qnw_ref[...]
    ckv = ckv_ref[...]
    ckvn = ckv * lax.rsqrt(jnp.mean(ckv * ckv, axis=-1, keepdims=True) + EPS) * kvnw_ref[...]
    cqn = cqn.astype(BF16)
    ckvn = ckvn.astype(BF16)
    k_pe = _rope(kr_ref[...], cos_t, sin_t).astype(BF16)
    scale = MLA_QK_DIM ** -0.5
    for h in range(MLA_HEADS):
        c0 = h * MLA_QK_PAD
        qh = _dot(cqn, wq_ref[:, c0:c0 + MLA_QK_PAD])
        q_ref[h, :, 0:MLA_NOPE] = (qh[:, :MLA_NOPE] * scale).astype(BF16)
        q_ref[h, :, MLA_NOPE:MLA_QK_PAD] = (_rope(qh[:, MLA_NOPE:], cos_t, sin_t) * scale).astype(BF16)
        kvh = _dot(ckvn, wkv_ref[:, c0:c0 + MLA_NOPE + MLA_VDIM])
        k_ref[h, :, 0:MLA_NOPE] = kvh[:, :MLA_NOPE].astype(BF16)
        k_ref[h, :, MLA_NOPE:MLA_QK_PAD] = k_pe
        v_ref[h] = kvh[:, MLA_NOPE:].astype(BF16)


def _mla_prep(cq, ckv, kr, positions, q_norm_w, w_uq, kv_norm_w, w_ukv, batch, seq, tm):
    t = cq.shape[0]
    half = MLA_ROPE // 2
    wq = w_uq.reshape(MLA_Q_RANK, MLA_HEADS, MLA_QK_DIM)
    zq = jnp.zeros((MLA_Q_RANK, MLA_HEADS, half), w_uq.dtype)
    wq = jnp.concatenate([wq[:, :, :MLA_NOPE], wq[:, :, MLA_NOPE:MLA_NOPE + half], zq,
                          wq[:, :, MLA_NOPE + half:], zq], axis=2)
    wq = wq.reshape(MLA_Q_RANK, MLA_HEADS * MLA_QK_PAD).astype(BF16)
    wkv = w_ukv.astype(BF16)
    inv_freq = ROPE_THETA ** (-jnp.arange(half, dtype=F32) / half)
    rot = jnp.tile(inv_freq, LANES // half).reshape(1, LANES)
    pos = positions.astype(F32).reshape(t, 1)
    nblk = seq // tm
    row = lambda width: pl.BlockSpec((tm, width), lambda i: (i, 0))
    full = lambda a: pl.BlockSpec(a.shape, lambda i: (0,) * a.ndim)
    hspec = lambda width: pl.BlockSpec((None, MLA_HEADS, tm, width), lambda i: (i // nblk, 0, i % nblk, 0))
    qnw = q_norm_w.reshape(1, -1)
    kvnw = kv_norm_w.reshape(1, -1)
    return pl.pallas_call(
        _mla_prep_kernel,
        grid=(t // tm,),
        in_specs=[row(MLA_Q_RANK), row(MLA_KV_RANK), row(LANES), row(1), full(qnw), full(kvnw),
                  full(wq), full(wkv), full(rot)],
        out_specs=[hspec(MLA_QK_PAD), hspec(MLA_QK_PAD), hspec(MLA_VDIM)],
        out_shape=[jax.ShapeDtypeStruct((batch, MLA_HEADS, seq, MLA_QK_PAD), BF16),
                   jax.ShapeDtypeStruct((batch, MLA_HEADS, seq, MLA_QK_PAD), BF16),
                   jax.ShapeDtypeStruct((batch, MLA_HEADS, seq, MLA_VDIM), BF16)],
        compiler_params=_cparams(("parallel",)),
        name="mla_prep",
    )(cq, ckv, kr, pos, qnw, kvnw, wq, wkv, rot)


def _flash_kernel(q_ref, k_ref, v_ref, o_ref, v1_scr, *, tq):
    heads, seq = q_ref.shape[0], q_ref.shape[1]
    row = lax.broadcasted_iota(I32, (tq, tq), 0)
    col = lax.broadcasted_iota(I32, (tq, tq), 1)
    causal = row >= col
    items = [(h, i) for h in range(heads) for i in range(seq // tq)]

    def scores(h, i):
        q = q_ref[h, i * tq:(i + 1) * tq, :]
        return _dot_nt(q, k_ref[h, 0:(i + 1) * tq, :])

    def softmax(h, i, s):
        lo = i * tq
        sd = jnp.where(causal, s[:, lo:lo + tq], -jnp.inf)
        m = jnp.max(sd, axis=-1, keepdims=True)
        if i > 0:
            m = jnp.maximum(m, jnp.max(s[:, 0:lo], axis=-1, keepdims=True))
        p = jnp.exp(jnp.concatenate([s[:, 0:lo], sd], axis=1) - m) if i > 0 else jnp.exp(sd - m)
        return p.astype(BF16)

    v1_scr[:, :, 0:MLA_VDIM] = v_ref[...]
    v1_scr[:, :, MLA_VDIM:] = jnp.ones((heads, seq, MLA_VDIM), BF16)

    def output(h, i, p):
        hi = (i + 1) * tq
        acc = _dot(p, v1_scr[h, 0:hi, :])
        o_ref[i * tq:hi, h * MLA_VDIM:(h + 1) * MLA_VDIM] = (acc[:, 0:MLA_VDIM] / acc[:, MLA_VDIM:]).astype(BF16)

    s = {n: scores(*items[n]) for n in range(min(2, len(items)))}
    pending = softmax(*items[0], s.pop(0))
    for n in range(len(items)):
        if n + 2 < len(items):
            s[n + 2] = scores(*items[n + 2])
        output(*items[n], pending)
        if n + 1 < len(items):
            pending = softmax(*items[n + 1], s.pop(n + 1))


def _flash(q, k, v, tq, heads_per_step):
    batch, heads, seq, _ = q.shape
    hp = heads_per_step
    qk = pl.BlockSpec((None, hp, seq, MLA_QK_PAD), lambda b, h: (b, h, 0, 0))
    return pl.pallas_call(
        functools.partial(_flash_kernel, tq=tq),
        grid=(batch, heads // hp),
        in_specs=[qk, qk, pl.BlockSpec((None, hp, seq, MLA_VDIM), lambda b, h: (b, h, 0, 0))],
        out_specs=pl.BlockSpec((None, seq, hp * MLA_VDIM), lambda b, h: (b, 0, h)),
        out_shape=jax.ShapeDtypeStruct((batch, seq, heads * MLA_VDIM), BF16),
        scratch_shapes=[pltpu.VMEM((hp, seq, 2 * MLA_VDIM), BF16)],
        compiler_params=_cparams(("parallel", "parallel")),
        name="flash",
    )(q, k, v)


_R_E0, _R_E1, _R_W0, _R_W1, _R_RANK0, _R_RANK1 = 0, 1, 2, 3, 4, 5


def _first_lane_equal(x, value, lane):
    return jnp.min(jnp.where(x == value, lane, LANES), axis=-1, keepdims=True)


def _combine_kernel(oa_ref, ob_ref, sa_ref, sb_ref, x_ref, wa_ref, wb_ref, wo_ref, fnw_ref, wr_ref, br_ref,
                    lt_ref, x1_ref, h2_ref, rec_ref, cnt_ref, run_ref, y_scr):
    i = pl.program_id(0)

    @pl.when(i == 0)
    def _():
        run_ref[...] = jnp.zeros_like(run_ref)

    sub = lt_ref.shape[0]
    n_sub = x_ref.shape[0] // sub
    half = x_ref.shape[1] // 2
    rows = lambda s: slice(s * sub, (s + 1) * sub)

    def branches(s):
        for c in (0, half):
            cols = slice(c, c + half)
            ya = _dot(oa_ref[rows(s), :], wa_ref[:, cols])
            yb = _dot(ob_ref[rows(s), :], wb_ref[:, cols])
            y = sa_ref[rows(s), cols].astype(F32) * ya + sb_ref[rows(s), cols].astype(F32) * yb
            y_scr[rows(s), cols] = y.astype(BF16)

    def residual(s):
        for c in (0, half):
            cols = slice(c, c + half)
            x1_ref[rows(s), cols] = x_ref[rows(s), cols] + _dot(y_scr[rows(s), :], wo_ref[:, cols])

    def router(s):
        x1 = x1_ref[rows(s), :]
        h2 = x1 * lax.rsqrt(jnp.mean(x1 * x1, axis=-1, keepdims=True) + EPS) * fnw_ref[...]
        h2_ref[rows(s), :] = h2
        hh, hl = _split_bf16(h2)
        return _dot(hh, wr_ref[0]) + _dot(hl, wr_ref[0]) + _dot(hh, wr_ref[1]) + br_ref[...]

    run = run_ref[...]
    logits = {}
    for step in range(n_sub + 3):
        if step < n_sub:
            branches(step)
        if 0 <= step - 1 < n_sub:
            residual(step - 1)
        if 0 <= step - 2 < n_sub:
            logits[step - 2] = router(step - 2)
        if 0 <= step - 3 < n_sub:
            rec, run = _route(logits.pop(step - 3), run, lt_ref[...])
            rec_ref[rows(step - 3), :] = rec
    run_ref[...] = run
    cnt_ref[...] = run


def _route(logits, run, lt):
    tm = logits.shape[0]
    lane = lax.broadcasted_iota(I32, (tm, LANES), 1)
    neg = -jnp.inf

    gl = jnp.where(lane < N_GROUPS, logits, neg)
    ge = jnp.exp(gl - jnp.max(gl, axis=-1, keepdims=True))
    gp = ge / jnp.sum(ge, axis=-1, keepdims=True)
    g_w = jnp.max(gp, axis=-1, keepdims=True)
    g_idx = _first_lane_equal(gp, g_w, lane)

    ej = lane - EXPERT_LANE0
    sel = (ej >= 0) & (ej < N_EXPERTS) & ((ej >> 3) == g_idx)
    el = jnp.where(sel, logits, neg)
    ee = jnp.exp(el - jnp.max(el, axis=-1, keepdims=True))
    ep = jnp.where(sel, ee / jnp.sum(ee, axis=-1, keepdims=True), -1.0)
    p0 = jnp.max(ep, axis=-1, keepdims=True)
    l0 = _first_lane_equal(ep, p0, lane)
    ep1 = jnp.where(lane == l0, -1.0, ep)
    p1 = jnp.max(ep1, axis=-1, keepdims=True)
    l1 = _first_lane_equal(ep1, p1, lane)
    psum = p0 + p1
    w0 = g_w * (p0 / psum)
    w1 = g_w * (p1 / psum)

    oh0 = lane == l0
    oh1 = lane == l1
    oh = jnp.where(oh0 | oh1, 1.0, 0.0)
    before = run + _dot(lt, oh.astype(BF16))
    rank0 = jnp.sum(jnp.where(oh0, before, 0.0), axis=-1, keepdims=True)
    rank1 = jnp.sum(jnp.where(oh1, before, 0.0), axis=-1, keepdims=True)
    run = run + jnp.sum(oh, axis=0, keepdims=True)

    rec = jnp.where(lane == _R_E0, (l0 - EXPERT_LANE0).astype(F32), 0.0)
    rec = jnp.where(lane == _R_E1, (l1 - EXPERT_LANE0).astype(F32), rec)
    rec = jnp.where(lane == _R_W0, w0, rec)
    rec = jnp.where(lane == _R_W1, w1, rec)
    rec = jnp.where(lane == _R_RANK0, rank0, rec)
    rec = jnp.where(lane == _R_RANK1, rank1, rec)
    return rec, run


def _combine(o_a, o_b, sa, sb, x2, w_a, w_b, w_o, ffn_norm_w, w_group, b_group, w_expert, b_expert, tm):
    t, d = x2.shape
    pad = LANES - N_GROUPS - N_EXPERTS
    wr = jnp.concatenate([w_group, w_expert, jnp.zeros((d, pad), F32)], axis=1)
    wr_hi = wr.astype(BF16)
    wr_lo = (wr - wr_hi.astype(F32)).astype(BF16)
    wr2 = jnp.stack([wr_hi, wr_lo])
    br = jnp.concatenate([b_group, b_expert, jnp.zeros((pad,), F32)]).reshape(1, LANES)
    lt = jnp.asarray(np.tril(np.ones((COMBINE_SUB, COMBINE_SUB), np.float32), -1), dtype=BF16)
    fnw = ffn_norm_w.reshape(1, d)
    row = lambda width: pl.BlockSpec((tm, width), lambda i: (i, 0))
    full = lambda a: pl.BlockSpec(a.shape, lambda i: (0,) * a.ndim)
    wa, wb, wo = w_a.astype(BF16), w_b.astype(BF16), w_o.astype(BF16)
    return pl.pallas_call(
        _combine_kernel,
        grid=(t // tm,),
        in_specs=[row(d), row(d), row(d), row(d), row(d), full(wa), full(wb), full(wo), full(fnw),
                  full(wr2), full(br), full(lt)],
        out_specs=[row(d), row(d), row(LANES), pl.BlockSpec((1, LANES), lambda i: (0, 0))],
        out_shape=[jax.ShapeDtypeStruct((t, d), F32),
                   jax.ShapeDtypeStruct((t, d), F32),
                   jax.ShapeDtypeStruct((t, LANES), F32),
                   jax.ShapeDtypeStruct((1, LANES), F32)],
        scratch_shapes=[pltpu.VMEM((1, LANES), F32), pltpu.VMEM((tm, d), BF16)],
        compiler_params=_cparams(("arbitrary",)),
        name="combine",
    )(o_a, o_b, sa, sb, x2, wa, wb, wo, fnw, wr2, br, lt)


def _row_copy(src_ref, src_row, dst_ref, dst_row, sem):
    return pltpu.make_async_copy(src_ref.at[pl.ds(src_row, 1)], dst_ref.at[pl.ds(dst_row, 1)], sem)


def _dest_kernel(rec_ref, ps_ref, d_ref):
    rec = rec_ref[...]
    lane = lax.broadcasted_iota(I32, rec.shape, 1)
    ps = ps_ref[...]

    def lookup(e_lane, r_lane):
        e = rec[:, e_lane:e_lane + 1].astype(I32) + EXPERT_LANE0
        start = jnp.sum(jnp.where(lane == e, ps, 0.0), axis=-1, keepdims=True)
        return (start + rec[:, r_lane:r_lane + 1]).astype(I32)

    d0 = lookup(_R_E0, _R_RANK0)
    d1 = lookup(_R_E1, _R_RANK1)
    d_ref[...] = jnp.where(lane == 0, d0, jnp.where(lane == 1, d1, 0))


def _dest(rec, pstart, tm):
    t = rec.shape[0]
    ps = jnp.zeros((1, LANES), F32).at[0, EXPERT_LANE0:EXPERT_LANE0 + N_EXPERTS].set(pstart.astype(F32))
    return pl.pallas_call(
        _dest_kernel,
        grid=(t // tm,),
        in_specs=[pl.BlockSpec((tm, LANES), lambda i: (i, 0)), pl.BlockSpec((1, LANES), lambda i: (0, 0))],
        out_specs=pl.BlockSpec((tm, LANES), lambda i: (i, 0)),
        out_shape=jax.ShapeDtypeStruct((t, LANES), I32),
        compiler_params=_cparams(("parallel",)),
        name="dest",
    )(rec, ps)


def _dispatch_kernel(ps_ref, cnt_ref, nu_ref, d_ref, h_ref, xs_ref, zero_scr, sem):
    tm = h_ref.shape[0]
    bm = zero_scr.shape[0]

    @pl.when(pl.program_id(0) == 0)
    def _():
        zero_scr[...] = jnp.zeros_like(zero_scr)

        def expert(e, carry):
            first = ps_ref[e] + cnt_ref[e]
            n_pad = (bm - (cnt_ref[e] & (bm - 1))) & (bm - 1)

            def start(r, c):
                _row_copy(zero_scr, 0, xs_ref, first + r, sem).start()
                return c

            def wait(r, c):
                _row_copy(zero_scr, 0, xs_ref, 0, sem).wait()
                return c

            lax.fori_loop(0, n_pad, start, 0)
            lax.fori_loop(0, n_pad, wait, 0)
            return carry

        lax.fori_loop(0, cnt_ref.shape[0], expert, 0)

        def tail(blk, carry):
            copy = pltpu.make_async_copy(zero_scr, xs_ref.at[pl.ds(blk * bm, bm)], sem)
            copy.start()
            copy.wait()
            return carry

        lax.fori_loop(nu_ref[0], xs_ref.shape[0] // bm, tail, 0)

    def start(j, carry):
        for u in range(DMA_UNROLL):
            tok = j * DMA_UNROLL + u
            for k in range(2):
                _row_copy(h_ref, tok, xs_ref, d_ref[2 * tok + k], sem).start(priority=k)
        return carry

    lax.fori_loop(0, tm // DMA_UNROLL, start, 0)

    def wait(j, carry):
        for _ in range(2 * DMA_UNROLL):
            _row_copy(h_ref, 0, xs_ref, 0, sem).wait()
        return carry

    lax.fori_loop(0, tm // DMA_UNROLL, wait, 0)


def _dispatch(h2, dest, pstart, counts, n_used, cap, tm):
    t, d = h2.shape
    return pl.pallas_call(
        _dispatch_kernel,
        grid_spec=pltpu.PrefetchScalarGridSpec(
            num_scalar_prefetch=3, grid=(t // tm,),
            in_specs=[pl.BlockSpec((2 * tm,), lambda i, ps, cnt, nu: (i,), memory_space=pltpu.SMEM),
                      pl.BlockSpec((tm, d), lambda i, ps, cnt, nu: (i, 0))],
            out_specs=pl.BlockSpec(memory_space=pl.ANY),
            scratch_shapes=[pltpu.VMEM((MOE_ROWS, d), h2.dtype), pltpu.SemaphoreType.DMA(())]),
        out_shape=jax.ShapeDtypeStruct((cap, d), h2.dtype),
        compiler_params=_cparams(("arbitrary",)),
        name="dispatch",
    )(pstart, counts, n_used, dest, h2)


def _experts_kernel(be_ref, nu_ref, xs_ref, w1_ref, w3_ref, w2_ref, ys_ref, w1b, w3b, w2b):
    i = pl.program_id(0)

    @pl.when(i < nu_ref[0])
    def _():
        changed = jnp.logical_or(i == 0, be_ref[i] != be_ref[jnp.maximum(i - 1, 0)])

        @pl.when(changed)
        def _():
            w1b[...] = w1_ref[...].astype(BF16)
            w3b[...] = w3_ref[...].astype(BF16)
            w2b[...] = w2_ref[...].astype(BF16)

        grp = xs_ref.shape[0] // MOE_ROW_GROUPS
        rows = lambda g: slice(g * grp, (g + 1) * grp)

        def up(g):
            x = xs_ref[rows(g), :].astype(BF16)
            return _dot(x, w1b[...]), _dot(x, w3b[...])

        def down(g, h1, h3):
            hid = h1 * jax.nn.sigmoid(h1) * h3
            ys_ref[rows(g), :] = _dot(hid.astype(BF16), w2b[...])

        nxt = up(0)
        for g in range(MOE_ROW_GROUPS):
            cur = nxt
            if g + 1 < MOE_ROW_GROUPS:
                nxt = up(g + 1)
            down(g, *cur)

    @pl.when(i >= nu_ref[0])
    def _():
        ys_ref[...] = jnp.zeros_like(ys_ref)


def _experts(xs, w1, w3, w2, block_expert, n_used):
    cap, d = xs.shape
    bm = MOE_ROWS
    ff = w1.shape[2]
    rows = pl.BlockSpec((bm, d), lambda i, be, nu: (jnp.minimum(i, nu[0] - 1), 0))
    return pl.pallas_call(
        _experts_kernel,
        grid_spec=pltpu.PrefetchScalarGridSpec(
            num_scalar_prefetch=2, grid=(cap // bm,),
            in_specs=[rows,
                      pl.BlockSpec((None, d, ff), lambda i, be, nu: (be[i], 0, 0)),
                      pl.BlockSpec((None, d, ff), lambda i, be, nu: (be[i], 0, 0)),
                      pl.BlockSpec((None, ff, d), lambda i, be, nu: (be[i], 0, 0))],
            out_specs=pl.BlockSpec((bm, d), lambda i, be, nu: (i, 0)),
            scratch_shapes=[pltpu.VMEM((d, ff), BF16), pltpu.VMEM((d, ff), BF16), pltpu.VMEM((ff, d), BF16)]),
        out_shape=jax.ShapeDtypeStruct((cap, d), F32),
        compiler_params=_cparams(("arbitrary",)),
        name="experts",
    )(block_expert, n_used, xs, w1, w3, w2)


def _final_kernel(d_ref, x1_ref, rec_ref, nw_ref, ys_ref, o_ref, buf, sem):
    tm = x1_ref.shape[0]

    def start(j, carry):
        for u in range(DMA_UNROLL):
            tok = j * DMA_UNROLL + u
            for k in range(2):
                _row_copy(ys_ref, d_ref[2 * tok + k], buf.at[k], tok, sem).start(priority=k)
        return carry

    lax.fori_loop(0, tm // DMA_UNROLL, start, 0)

    def wait(j, carry):
        for _ in range(2 * DMA_UNROLL):
            _row_copy(ys_ref, 0, buf.at[0], 0, sem).wait()
        return carry

    lax.fori_loop(0, tm // DMA_UNROLL, wait, 0)

    rec = rec_ref[...]
    x = x1_ref[...] + buf[0] * rec[:, _R_W0:_R_W0 + 1] + buf[1] * rec[:, _R_W1:_R_W1 + 1]
    o_ref[...] = x * lax.rsqrt(jnp.mean(x * x, axis=-1, keepdims=True) + EPS) * nw_ref[...]


def _final(x1, rec, ys, dest, final_norm_w, tm):
    t, d = x1.shape
    nw = final_norm_w.reshape(1, d)
    return pl.pallas_call(
        _final_kernel,
        grid=(t // tm,),
        in_specs=[pl.BlockSpec((2 * tm,), lambda i: (i,), memory_space=pltpu.SMEM),
                  pl.BlockSpec((tm, d), lambda i: (i, 0)),
                  pl.BlockSpec((tm, LANES), lambda i: (i, 0)),
                  pl.BlockSpec((1, d), lambda i: (0, 0)),
                  pl.BlockSpec(memory_space=pl.ANY)],
        out_specs=pl.BlockSpec((tm, d), lambda i: (i, 0)),
        out_shape=jax.ShapeDtypeStruct((t, d), F32),
        scratch_shapes=[pltpu.VMEM((2, tm, d), F32), pltpu.SemaphoreType.DMA(())],
        compiler_params=_cparams(("arbitrary",)),
        name="final",
    )(dest, x1, rec, nw, ys)


def _moe_layout(counts):
    bm = MOE_ROWS
    padded = (counts + bm - 1) // bm * bm
    pends = jnp.cumsum(padded)
    pstart = pends - padded
    return pstart.astype(I32), pends.astype(I32)


def kernel(x, positions, attn_norm_w, w_in, hg_lower_bound, hg_out_norm_w, mla_q_norm_w, mla_w_uq, mla_kv_norm_w, mla_w_ukv, w_branch_hgrn, w_branch_mla, w_out, ffn_norm_w, router_group_w, router_group_b, router_expert_w, router_expert_b, expert_w1, expert_w3, expert_w2, final_norm_w):
    batch, seq, d = x.shape
    t = batch * seq
    depth = w_in.shape[0]
    assert d == 1024 and seq % 512 == 0 and t % 1024 == 0
    x2 = x.reshape(t, d)
    for l in range(depth):
        q, kk, lf, v, g, cq, ckv, kr, sa, sb = _in_proj(x2, attn_norm_w[l], w_in[l], hg_lower_bound, l, tm=256)
        o_a = _hgrn(q, kk, lf, v, g, hg_out_norm_w[l], batch, seq)
        mq, mk, mv = _mla_prep(cq, ckv, kr, positions, mla_q_norm_w[l], mla_w_uq[l], mla_kv_norm_w[l],
                               mla_w_ukv[l], batch, seq, tm=512)
        o_b = _flash(mq, mk, mv, tq=256, heads_per_step=2).reshape(t, MLA_HEADS * MLA_VDIM)
        x1, h2, rec, cnt = _combine(o_a, o_b, sa, sb, x2, w_branch_hgrn[l], w_branch_mla[l], w_out[l],
                                    ffn_norm_w[l], router_group_w[l], router_group_b[l],
                                    router_expert_w[l], router_expert_b[l], tm=1024)

        counts = cnt[0, EXPERT_LANE0:EXPERT_LANE0 + N_EXPERTS].astype(I32)
        pstart, pends = _moe_layout(counts)
        n_blocks = (2 * t) // MOE_ROWS + N_EXPERTS
        cap = n_blocks * MOE_ROWS
        n_used = (pends[-1] // MOE_ROWS).reshape(1)
        blk = jnp.minimum(jnp.arange(n_blocks, dtype=I32), n_used[0] - 1)
        be = jnp.sum((blk[:, None] * MOE_ROWS >= pends[None, :]).astype(I32), axis=1)
        dest = _dest(rec, pstart, tm=math.gcd(t, 2048))[:, 0:2].reshape(2 * t)

        xs = _dispatch(h2, dest, pstart, counts, n_used, cap, tm=512)
        ys = _experts(xs, expert_w1[l], expert_w3[l], expert_w2[l], be, n_used)
        assert l == depth - 1, "multi-layer stacking needs an un-normalised residual output"
        out = _final(x1, rec, ys, dest, final_norm_w, tm=512)
    return out.reshape(batch, seq, d)
```

```python
import functools
import math

import numpy as np
import jax
import jax.numpy as jnp
from jax import lax
from jax.experimental import pallas as pl
from jax.experimental.pallas import tpu as pltpu

F32 = jnp.float32
BF16 = jnp.bfloat16
I32 = jnp.int32

EPS = 1e-6
LANES = 128
HG_HEADS = 8
HG_DIM = 128
MLA_HEADS = 8
MLA_Q_RANK = 384
MLA_KV_RANK = 256
MLA_NOPE = 128
MLA_ROPE = 64
MLA_VDIM = 128
MLA_QK_DIM = MLA_NOPE + MLA_ROPE
MLA_QK_PAD = 256
ROPE_THETA = 10000.0
N_GROUPS = 8
EXPERTS_PER_GROUP = 8
N_EXPERTS = N_GROUPS * EXPERTS_PER_GROUP
EXPERT_LANE0 = N_GROUPS
EXPERT_FF = 512

HG_CHUNK = 128
HG_LOW_BLOCKS = (1, 2, 4)
HG_HIGH_BLOCKS = (8, 16, 32, 64)
HG_HEADS_PER_STEP = 2
HG_CHUNK_UNROLL = 8
COMBINE_SUB = 256
MOE_ROWS = 256
MOE_ROW_GROUPS = 2
DMA_UNROLL = 8
VMEM_LIMIT = 56 * 1024 * 1024


def _cparams(sem):
    return pltpu.CompilerParams(dimension_semantics=sem, vmem_limit_bytes=VMEM_LIMIT)


def _dot(a, b):
    return jnp.dot(a, b, preferred_element_type=F32)


def _dot_nt(a, b):
    return lax.dot_general(a, b, (((1,), (1,)), ((), ())), preferred_element_type=F32)


def _dot_tn(a, b):
    return lax.dot_general(a, b, (((0,), (0,)), ((), ())), preferred_element_type=F32)


def _split_bf16(x):
    hi = x.astype(BF16)
    lo = (x - hi.astype(F32)).astype(BF16)
    return hi, lo


_IN_OFF_Q = 0
_IN_OFF_F = 1024
_IN_OFF_I = 2048
_IN_OFF_G = 3072
_IN_OFF_CQ = 4096
_IN_OFF_CKV = _IN_OFF_CQ + MLA_Q_RANK
_IN_OFF_KR = _IN_OFF_CKV + MLA_KV_RANK
_IN_OFF_GA = _IN_OFF_KR + LANES
_IN_OFF_GB = _IN_OFF_GA + 1024
_IN_COLS_PAD = _IN_OFF_GB + 1024


def _rope_lane_layout(w):
    half = MLA_ROPE // 2
    z = jnp.zeros((w.shape[0], half), w.dtype)
    return jnp.concatenate([w[:, :half], z, w[:, half:], z], axis=1)


def _inproj_kernel(x_ref, nw_ref, lbt_ref, w_ref, q_ref, kk_ref, lf_ref, v_ref, g_ref, cq_ref, ckv_ref,
                   kr_ref, sa_ref, sb_ref, h_scr, *, layer):
    x = x_ref[...]
    ms = jnp.mean(x * x, axis=-1, keepdims=True)
    h_scr[...] = (x * lax.rsqrt(ms + EPS) * nw_ref[...]).astype(BF16)

    tab = lbt_ref[...]
    e = jnp.exp(tab - jnp.max(tab, axis=0, keepdims=True))
    sm = e / jnp.sum(e, axis=0, keepdims=True)
    lb = jnp.sum(sm[:layer + 1], axis=0, keepdims=True)

    def proj(off, width):
        return _dot(h_scr[...], w_ref[:, off:off + width])

    half = 512
    for c in range(0, 1024, half):
        z = proj(_IN_OFF_Q + c, half)
        q_ref[:, c:c + half] = (z * jax.nn.sigmoid(z) * (HG_DIM ** -0.5)).astype(BF16)
        z = proj(_IN_OFF_F + c, half)
        lbc = lb[:, c:c + half]
        fg = lbc + (1.0 - lbc) * jax.nn.sigmoid(z)
        lf_ref[:, c:c + half] = jnp.log(fg)
        kk_ref[:, c:c + half] = (1.0 - fg).astype(BF16)
        v_ref[:, c:c + half] = proj(_IN_OFF_I + c, half).astype(BF16)
        z = proj(_IN_OFF_G + c, half)
        g_ref[:, c:c + half] = (z * jax.nn.sigmoid(z)).astype(BF16)
        sa_ref[:, c:c + half] = jax.nn.sigmoid(proj(_IN_OFF_GA + c, half)).astype(BF16)
        sb_ref[:, c:c + half] = jax.nn.sigmoid(proj(_IN_OFF_GB + c, half)).astype(BF16)
    cq_ref[...] = proj(_IN_OFF_CQ, MLA_Q_RANK)
    ckv_ref[...] = proj(_IN_OFF_CKV, MLA_KV_RANK)
    kr_ref[...] = proj(_IN_OFF_KR, LANES)


def _in_proj(x2, attn_norm_w, w_in, lb_table, layer, tm):
    t, d = x2.shape
    parts = []
    off = 0
    for width in (1024, 1024, 1024, 1024, MLA_Q_RANK, MLA_KV_RANK, MLA_ROPE, 1024, 1024):
        parts.append(w_in[:, off:off + width])
        off += width
    parts[6] = _rope_lane_layout(parts[6])
    w = jnp.concatenate(parts, axis=1).astype(BF16)
    assert w.shape[1] == _IN_COLS_PAD
    row = lambda width: pl.BlockSpec((tm, width), lambda i: (i, 0))
    full = lambda a: pl.BlockSpec(a.shape, lambda i: (0,) * a.ndim)
    nw = attn_norm_w.reshape(1, d)
    out_shape = [jax.ShapeDtypeStruct((t, 1024), BF16),
                 jax.ShapeDtypeStruct((t, 1024), BF16),
                 jax.ShapeDtypeStruct((t, 1024), F32),
                 jax.ShapeDtypeStruct((t, 1024), BF16),
                 jax.ShapeDtypeStruct((t, 1024), BF16),
                 jax.ShapeDtypeStruct((t, MLA_Q_RANK), F32),
                 jax.ShapeDtypeStruct((t, MLA_KV_RANK), F32),
                 jax.ShapeDtypeStruct((t, LANES), F32),
                 jax.ShapeDtypeStruct((t, 1024), BF16),
                 jax.ShapeDtypeStruct((t, 1024), BF16)]
    return pl.pallas_call(
        functools.partial(_inproj_kernel, layer=layer),
        grid=(t // tm,),
        in_specs=[row(d), full(nw), full(lb_table),
                  pl.BlockSpec(w.shape, lambda i: (0, 0), pipeline_mode=pl.Buffered(1))],
        out_specs=[row(s.shape[1]) for s in out_shape],
        out_shape=out_shape,
        scratch_shapes=[pltpu.VMEM((tm, d), BF16)],
        compiler_params=_cparams(("parallel",)),
        name="in_proj",
    )(x2, nw, lb_table, w)


def _hgrn_tables():
    n = HG_CHUNK
    r = np.arange(n)[:, None]
    j = np.arange(n)[None, :]
    mats = [j <= r]
    for m in HG_LOW_BLOCKS:
        c = (r // (2 * m)) * (2 * m) + m - 1
        odd = (r & m) != 0
        mats.append(np.where(odd, (j > c) & (j <= r), (j > r) & (j <= c)))
    tab = np.concatenate(mats, axis=0).astype(np.float32)
    return jnp.asarray(np.concatenate([tab, tab], axis=1), dtype=BF16)


def _hgrn_kernel(q_ref, kk_ref, lf_ref, v_ref, g_ref, nw_ref, tab_ref, o_ref, st_ref, *, heads, n_chunks):
    n = HG_CHUNK
    unroll = math.gcd(n_chunks, HG_CHUNK_UNROLL)
    assert n == HG_DIM
    row = lax.broadcasted_iota(I32, (n, n), 0)
    col = lax.broadcasted_iota(I32, (n, n), 1)
    code = jnp.where(col <= row, 32 - lax.clz(row ^ col), -1)
    st_ref[...] = jnp.zeros_like(st_ref)
    nw = nw_ref[...]

    def load(r0, h):
        ls = slice(h * HG_DIM, (h + 1) * HG_DIM)
        qb = q_ref[pl.ds(r0, n), ls]
        kb = kk_ref[pl.ds(r0, n), ls]
        hi, lo = _split_bf16(lf_ref[pl.ds(r0, n), ls])
        ex = _dot(tab_ref[...], jnp.concatenate([hi, lo], axis=0))
        return dict(r0=r0, h=h, ls=ls, qb=qb, kb=kb, vb=v_ref[pl.ds(r0, n), ls], ex=ex)

    def low_products(p):
        q, k, ex = p["qb"].astype(F32), p["kb"].astype(F32), p["ex"]
        prods = [_dot_nt(p["qb"], p["kb"])]
        for i, m in enumerate(HG_LOW_BLOCKS):
            odd = (row & m) != 0
            w = jnp.where(odd, q, k) * jnp.exp(ex[(i + 1) * n:(i + 2) * n, :])
            prods.append(_dot_nt(jnp.where(odd, w, 0.0).astype(BF16), jnp.where(odd, 0.0, w).astype(BF16)))
        p["low"] = prods

    def high_products(p):
        q, k, b = p["qb"].astype(F32), p["kb"].astype(F32), p["ex"][0:n, :]
        prods = []
        for m in HG_HIGH_BLOCKS:
            bases = range(0, n, 2 * m)
            e_parts, s_parts = [], []
            for base in bases:
                mid = b[base + m - 1:base + m, :]
                e_parts += [mid - b[base:base + m, :], b[base + m:base + 2 * m, :] - mid]
                s_parts += [k[base:base + m, :], q[base + m:base + 2 * m, :]]
            w = jnp.concatenate(s_parts, axis=0) * jnp.exp(jnp.concatenate(e_parts, axis=0))
            zero = jnp.zeros((m, HG_DIM), F32)
            kp = jnp.concatenate([x for base in bases for x in (w[base:base + m, :], zero)], axis=0)
            qc = jnp.concatenate([w[base + m:base + 2 * m, :] for base in bases], axis=0)
            prods.append(_dot_nt(qc.astype(BF16), kp.astype(BF16)))
        p["high"] = prods

    def intra(p):
        a = jnp.where(code == 0, p["low"][0], 0.0)
        for m, r in zip(HG_LOW_BLOCKS, p["low"][1:]):
            a = jnp.where(code == m.bit_length(), r, a)
        for m, r in zip(HG_HIGH_BLOCKS, p["high"]):
            pieces = []
            for j, base in enumerate(range(0, n, 2 * m)):
                second = slice(base + m, base + 2 * m)
                pieces += [a[base:base + m, :],
                           jnp.where(code[second, :] == m.bit_length(), r[j * m:(j + 1) * m, :], a[second, :])]
            a = jnp.concatenate(pieces, axis=0)
        q, k, b = p["qb"].astype(F32), p["kb"].astype(F32), p["ex"][0:n, :]
        b_last = b[n - 1:n, :]
        p["o"] = _dot(a.astype(BF16), p["vb"])
        p["qe"] = (q * jnp.exp(b)).astype(BF16)
        p["upd"] = _dot_tn(p["vb"], (k * jnp.exp(b_last - b)).astype(BF16))
        p["decay"] = jnp.exp(b_last)

    def finish(p, st):
        o = p["o"] + _dot_nt(p["qe"], st.astype(BF16))
        ms = jnp.mean(o * o, axis=-1, keepdims=True)
        y = o * lax.rsqrt(ms + EPS) * nw
        o_ref[pl.ds(p["r0"], n), p["ls"]] = (y * g_ref[pl.ds(p["r0"], n), p["ls"]].astype(F32)).astype(BF16)
        return st * p["decay"] + p["upd"]

    stages = (low_products, high_products, intra)

    def chunk_group(cg, carry):
        pairs = [(u, h) for u in range(unroll) for h in range(heads)]
        state = [st_ref[h] for h in range(heads)]
        live = {}
        for step in range(len(pairs) + len(stages) + 1):
            if step < len(pairs):
                u, h = pairs[step]
                live[step] = load(pl.multiple_of((cg * unroll + u) * n, n), h)
            for d, stage in enumerate(stages, start=1):
                if 0 <= step - d < len(pairs):
                    stage(live[step - d])
            done = step - len(stages) - 1
            if 0 <= done < len(pairs):
                h = pairs[done][1]
                state[h] = finish(live.pop(done), state[h])
        for h in range(heads):
            st_ref[h] = state[h]
        return carry

    lax.fori_loop(0, n_chunks // unroll, chunk_group, 0)


def _hgrn(q, kk, lf, v, g, out_norm_w, batch, seq):
    t, width = q.shape
    hp = HG_HEADS_PER_STEP
    wblk = hp * HG_DIM
    tab = _hgrn_tables()
    spec = pl.BlockSpec((seq, wblk), lambda b, j: (b, j))
    full = lambda a: pl.BlockSpec(a.shape, lambda b, j: (0,) * a.ndim)
    nw = out_norm_w.reshape(1, HG_DIM)
    return pl.pallas_call(
        functools.partial(_hgrn_kernel, heads=hp, n_chunks=seq // HG_CHUNK),
        grid=(batch, width // wblk),
        in_specs=[spec, spec, spec, spec, spec, full(nw), full(tab)],
        out_specs=spec,
        out_shape=jax.ShapeDtypeStruct((t, width), BF16),
        scratch_shapes=[pltpu.VMEM((hp, HG_DIM, HG_DIM), F32)],
        compiler_params=_cparams(("parallel", "parallel")),
        name="hgrn",
    )(q, kk, lf, v, g, nw, tab)


def _rope(p, cos_t, sin_t):
    return p * cos_t + pltpu.roll(p, LANES // 2, axis=1) * sin_t


def _rope_tables(pos, freq):
    rows = pos.shape[0]
    qr = rows // 4
    half = MLA_ROPE // 2
    lane = lax.broadcasted_iota(I32, (qr, LANES), 1)
    packed = jnp.where(lane < half, pos[0:qr], jnp.where(lane < 2 * half, pos[qr:2 * qr],
                       jnp.where(lane < 3 * half, pos[2 * qr:3 * qr], pos[3 * qr:])))
    ang = packed * freq
    cos_p, sin_p = jnp.cos(ang), jnp.sin(ang)
    first = lane < half
    second = (lane >= 2 * half) & (lane < 3 * half)
    roll = lambda x, shift: x if shift % LANES == 0 else pltpu.roll(x, shift % LANES, axis=1)
    cos_parts, sin_parts = [], []
    for j in range(4):
        to_first, to_second = -half * j, 2 * half - half * j
        cos_parts.append(jnp.where(first, roll(cos_p, to_first), jnp.where(second, roll(cos_p, to_second), 0.0)))
        sin_parts.append(jnp.where(first, -roll(sin_p, to_first), jnp.where(second, roll(sin_p, to_second), 0.0)))
    return jnp.concatenate(cos_parts, axis=0), jnp.concatenate(sin_parts, axis=0)


def _mla_prep_kernel(cq_ref, ckv_ref, kr_ref, pos_ref, qnw_ref, kvnw_ref, wq_ref, wkv_ref, rot_ref,
                     q_ref, k_ref, v_ref):
    cos_t, sin_t = _rope_tables(pos_ref[...], rot_ref[...])

    cq = cq_ref[...]
    cqn = cq * lax.rsqrt(jnp.mean(cq * cq, axis=-1, keepdims=True) + EPS) * qnw_ref[...]
    ckv = ckv_ref[...]
    ckvn = ckv * lax.rsqrt(jnp.mean(ckv * ckv, axis=-1, keepdims=True) + EPS) * kvnw_ref[...]
    cqn = cqn.astype(BF16)
    ckvn = ckvn.astype(BF16)
    k_pe = _rope(kr_ref[...], cos_t, sin_t).astype(BF16)
    scale = MLA_QK_DIM ** -0.5
    for h in range(MLA_HEADS):
        c0 = h * MLA_QK_PAD
        qh = _dot(cqn, wq_ref[:, c0:c0 + MLA_QK_PAD])
        q_ref[h, :, 0:MLA_NOPE] = (qh[:, :MLA_NOPE] * scale).astype(BF16)
        q_ref[h, :, MLA_NOPE:MLA_QK_PAD] = (_rope(qh[:, MLA_NOPE:], cos_t, sin_t) * scale).astype(BF16)
        kvh = _dot(ckvn, wkv_ref[:, c0:c0 + MLA_NOPE + MLA_VDIM])
        k_ref[h, :, 0:MLA_NOPE] = kvh[:, :MLA_NOPE].astype(BF16)
        k_ref[h, :, MLA_NOPE:MLA_QK_PAD] = k_pe
        v_ref[h] = kvh[:, MLA_NOPE:].astype(BF16)


def _mla_prep(cq, ckv, kr, positions, q_norm_w, w_uq, kv_norm_w, w_ukv, batch, seq, tm):
    t = cq.shape[0]
    half = MLA_ROPE // 2
    wq = w_uq.reshape(MLA_Q_RANK, MLA_HEADS, MLA_QK_DIM)
    zq = jnp.zeros((MLA_Q_RANK, MLA_HEADS, half), w_uq.dtype)
    wq = jnp.concatenate([wq[:, :, :MLA_NOPE], wq[:, :, MLA_NOPE:MLA_NOPE + half], zq,
                          wq[:, :, MLA_NOPE + half:], zq], axis=2)
    wq = wq.reshape(MLA_Q_RANK, MLA_HEADS * MLA_QK_PAD).astype(BF16)
    wkv = w_ukv.astype(BF16)
    inv_freq = ROPE_THETA ** (-jnp.arange(half, dtype=F32) / half)
    rot = jnp.tile(inv_freq, LANES // half).reshape(1, LANES)
    pos = positions.astype(F32).reshape(t, 1)
    nblk = seq // tm
    row = lambda width: pl.BlockSpec((tm, width), lambda i: (i, 0))
    full = lambda a: pl.BlockSpec(a.shape, lambda i: (0,) * a.ndim)
    hspec = lambda width: pl.BlockSpec((None, MLA_HEADS, tm, width), lambda i: (i // nblk, 0, i % nblk, 0))
    qnw = q_norm_w.reshape(1, -1)
    kvnw = kv_norm_w.reshape(1, -1)
    return pl.pallas_call(
        _mla_prep_kernel,
        grid=(t // tm,),
        in_specs=[row(MLA_Q_RANK), row(MLA_KV_RANK), row(LANES), row(1), full(qnw), full(kvnw),
                  full(wq), full(wkv), full(rot)],
        out_specs=[hspec(MLA_QK_PAD), hspec(MLA_QK_PAD), hspec(MLA_VDIM)],
        out_shape=[jax.ShapeDtypeStruct((batch, MLA_HEADS, seq, MLA_QK_PAD), BF16),
                   jax.ShapeDtypeStruct((batch, MLA_HEADS, seq, MLA_QK_PAD), BF16),
                   jax.ShapeDtypeStruct((batch, MLA_HEADS, seq, MLA_VDIM), BF16)],
        compiler_params=_cparams(("parallel",)),
        name="mla_prep",
    )(cq, ckv, kr, pos, qnw, kvnw, wq, wkv, rot)


def _flash_kernel(q_ref, k_ref, v_ref, o_ref, v1_scr, *, tq):
    heads, seq = q_ref.shape[0], q_ref.shape[1]
    row = lax.broadcasted_iota(I32, (tq, tq), 0)
    col = lax.broadcasted_iota(I32, (tq, tq), 1)
    causal = row >= col
    items = [(h, i) for h in range(heads) for i in range(seq // tq)]

    def scores(h, i):
        q = q_ref[h, i * tq:(i + 1) * tq, :]
        return _dot_nt(q, k_ref[h, 0:(i + 1) * tq, :])

    def softmax(h, i, s):
        lo = i * tq
        sd = jnp.where(causal, s[:, lo:lo + tq], -jnp.inf)
        m = jnp.max(sd, axis=-1, keepdims=True)
        if i > 0:
            m = jnp.maximum(m, jnp.max(s[:, 0:lo], axis=-1, keepdims=True))
        p = jnp.exp(jnp.concatenate([s[:, 0:lo], sd], axis=1) - m) if i > 0 else jnp.exp(sd - m)
        return p.astype(BF16)

    v1_scr[:, :, 0:MLA_VDIM] = v_ref[...]
    v1_scr[:, :, MLA_VDIM:] = jnp.ones((heads, seq, MLA_VDIM), BF16)

    def output(h, i, p):
        hi = (i + 1) * tq
        acc = _dot(p, v1_scr[h, 0:hi, :])
        o_ref[i * tq:hi, h * MLA_VDIM:(h + 1) * MLA_VDIM] = (acc[:, 0:MLA_VDIM] / acc[:, MLA_VDIM:]).astype(BF16)

    s = {n: scores(*items[n]) for n in range(min(2, len(items)))}
    pending = softmax(*items[0], s.pop(0))
    for n in range(len(items)):
        if n + 2 < len(items):
            s[n + 2] = scores(*items[n + 2])
        output(*items[n], pending)
        if n + 1 < len(items):
            pending = softmax(*items[n + 1], s.pop(n + 1))


def _flash(q, k, v, tq, heads_per_step):
    batch, heads, seq, _ = q.shape
    hp = heads_per_step
    qk = pl.BlockSpec((None, hp, seq, MLA_QK_PAD), lambda b, h: (b, h, 0, 0))
    return pl.pallas_call(
        functools.partial(_flash_kernel, tq=tq),
        grid=(batch, heads // hp),
        in_specs=[qk, qk, pl.BlockSpec((None, hp, seq, MLA_VDIM), lambda b, h: (b, h, 0, 0))],
        out_specs=pl.BlockSpec((None, seq, hp * MLA_VDIM), lambda b, h: (b, 0, h)),
        out_shape=jax.ShapeDtypeStruct((batch, seq, heads * MLA_VDIM), BF16),
        scratch_shapes=[pltpu.VMEM((hp, seq, 2 * MLA_VDIM), BF16)],
        compiler_params=_cparams(("parallel", "parallel")),
        name="flash",
    )(q, k, v)


_R_E0, _R_E1, _R_W0, _R_W1, _R_RANK0, _R_RANK1 = 0, 1, 2, 3, 4, 5


def _first_lane_equal(x, value, lane):
    return jnp.min(jnp.where(x == value, lane, LANES), axis=-1, keepdims=True)


def _combine_kernel(oa_ref, ob_ref, sa_ref, sb_ref, x_ref, wa_ref, wb_ref, wo_ref, fnw_ref, wr_ref, br_ref,
                    lt_ref, x1_ref, h2_ref, rec_ref, cnt_ref, run_ref, y_scr):
    i = pl.program_id(0)

    @pl.when(i == 0)
    def _():
        run_ref[...] = jnp.zeros_like(run_ref)

    sub = lt_ref.shape[0]
    n_sub = x_ref.shape[0] // sub
    half = x_ref.shape[1] // 2
    rows = lambda s: slice(s * sub, (s + 1) * sub)

    def branches(s):
        for c in (0, half):
            cols = slice(c, c + half)
            ya = _dot(oa_ref[rows(s), :], wa_ref[:, cols])
            yb = _dot(ob_ref[rows(s), :], wb_ref[:, cols])
            y = sa_ref[rows(s), cols].astype(F32) * ya + sb_ref[rows(s), cols].astype(F32) * yb
            y_scr[rows(s), cols] = y.astype(BF16)

    def residual(s):
        for c in (0, half):
            cols = slice(c, c + half)
            x1_ref[rows(s), cols] = x_ref[rows(s), cols] + _dot(y_scr[rows(s), :], wo_ref[:, cols])

    def router(s):
        x1 = x1_ref[rows(s), :]
        h2 = x1 * lax.rsqrt(jnp.mean(x1 * x1, axis=-1, keepdims=True) + EPS) * fnw_ref[...]
        h2_ref[rows(s), :] = h2
        hh, hl = _split_bf16(h2)
        both = _dot(hh, wr_ref[...])
        return both[:, 0:LANES] + both[:, LANES:] + _dot(hl, wr_ref[:, 0:LANES]) + br_ref[...]

    run = run_ref[...]
    logits = {}
    for step in range(n_sub + 3):
        if step < n_sub:
            branches(step)
        if 0 <= step - 1 < n_sub:
            residual(step - 1)
        if 0 <= step - 2 < n_sub:
            logits[step - 2] = router(step - 2)
        if 0 <= step - 3 < n_sub:
            rec, run = _route(logits.pop(step - 3), run, lt_ref[...])
            rec_ref[rows(step - 3), :] = rec
    run_ref[...] = run
    cnt_ref[...] = run


def _route(logits, run, lt):
    tm = logits.shape[0]
    lane = lax.broadcasted_iota(I32, (tm, LANES), 1)
    neg = -jnp.inf

    gl = jnp.where(lane < N_GROUPS, logits, neg)
    ge = jnp.exp(gl - jnp.max(gl, axis=-1, keepdims=True))
    gp = ge / jnp.sum(ge, axis=-1, keepdims=True)
    g_w = jnp.max(gp, axis=-1, keepdims=True)
    g_idx = _first_lane_equal(gp, g_w, lane)

    ej = lane - EXPERT_LANE0
    sel = (ej >= 0) & (ej < N_EXPERTS) & ((ej >> 3) == g_idx)
    el = jnp.where(sel, logits, neg)
    ee = jnp.exp(el - jnp.max(el, axis=-1, keepdims=True))
    ep = jnp.where(sel, ee / jnp.sum(ee, axis=-1, keepdims=True), -1.0)
    p0 = jnp.max(ep, axis=-1, keepdims=True)
    l0 = _first_lane_equal(ep, p0, lane)
    ep1 = jnp.where(lane == l0, -1.0, ep)
    p1 = jnp.max(ep1, axis=-1, keepdims=True)
    l1 = _first_lane_equal(ep1, p1, lane)
    psum = p0 + p1
    w0 = g_w * (p0 / psum)
    w1 = g_w * (p1 / psum)

    oh0 = lane == l0
    oh1 = lane == l1
    oh = jnp.where(oh0 | oh1, 1.0, 0.0)
    before = run + _dot(lt, oh.astype(BF16))
    rank0 = jnp.sum(jnp.where(oh0, before, 0.0), axis=-1, keepdims=True)
    rank1 = jnp.sum(jnp.where(oh1, before, 0.0), axis=-1, keepdims=True)
    run = run + jnp.sum(oh, axis=0, keepdims=True)

    rec = jnp.where(lane == _R_E0, (l0 - EXPERT_LANE0).astype(F32), 0.0)
    rec = jnp.where(lane == _R_E1, (l1 - EXPERT_LANE0).astype(F32), rec)
    rec = jnp.where(lane == _R_W0, w0, rec)
    rec = jnp.where(lane == _R_W1, w1, rec)
    rec = jnp.where(lane == _R_RANK0, rank0, rec)
    rec = jnp.where(lane == _R_RANK1, rank1, rec)
    return rec, run


def _combine(o_a, o_b, sa, sb, x2, w_a, w_b, w_o, ffn_norm_w, w_group, b_group, w_expert, b_expert, tm):
    t, d = x2.shape
    pad = LANES - N_GROUPS - N_EXPERTS
    wr = jnp.concatenate([w_group, w_expert, jnp.zeros((d, pad), F32)], axis=1)
    wr_hi = wr.astype(BF16)
    wr_lo = (wr - wr_hi.astype(F32)).astype(BF16)
    wr2 = jnp.concatenate([wr_hi, wr_lo], axis=1)
    br = jnp.concatenate([b_group, b_expert, jnp.zeros((pad,), F32)]).reshape(1, LANES)
    lt = jnp.asarray(np.tril(np.ones((COMBINE_SUB, COMBINE_SUB), np.float32), -1), dtype=BF16)
    fnw = ffn_norm_w.reshape(1, d)
    row = lambda width: pl.BlockSpec((tm, width), lambda i: (i, 0))
    full = lambda a: pl.BlockSpec(a.shape, lambda i: (0,) * a.ndim)
    wa, wb, wo = w_a.astype(BF16), w_b.astype(BF16), w_o.astype(BF16)
    return pl.pallas_call(
        _combine_kernel,
        grid=(t // tm,),
        in_specs=[row(d), row(d), row(d), row(d), row(d), full(wa), full(wb), full(wo), full(fnw),
                  full(wr2), full(br), full(lt)],
        out_specs=[row(d), row(d), row(LANES), pl.BlockSpec((1, LANES), lambda i: (0, 0))],
        out_shape=[jax.ShapeDtypeStruct((t, d), F32),
                   jax.ShapeDtypeStruct((t, d), F32),
                   jax.ShapeDtypeStruct((t, LANES), F32),
                   jax.ShapeDtypeStruct((1, LANES), F32)],
        scratch_shapes=[pltpu.VMEM((1, LANES), F32), pltpu.VMEM((tm, d), BF16)],
        compiler_params=_cparams(("arbitrary",)),
        name="combine",
    )(o_a, o_b, sa, sb, x2, wa, wb, wo, fnw, wr2, br, lt)


def _row_copy(src_ref, src_row, dst_ref, dst_row, sem):
    return pltpu.make_async_copy(src_ref.at[pl.ds(src_row, 1)], dst_ref.at[pl.ds(dst_row, 1)], sem)


def _dest_kernel(rec_ref, ps_ref, d_ref):
    rec = rec_ref[...]
    lane = lax.broadcasted_iota(I32, rec.shape, 1)
    ps = ps_ref[...]

    def lookup(e_lane, r_lane):
        e = rec[:, e_lane:e_lane + 1].astype(I32) + EXPERT_LANE0
        start = jnp.sum(jnp.where(lane == e, ps, 0.0), axis=-1, keepdims=True)
        return (start + rec[:, r_lane:r_lane + 1]).astype(I32)

    d0 = lookup(_R_E0, _R_RANK0)
    d1 = lookup(_R_E1, _R_RANK1)
    d_ref[...] = jnp.where(lane == 0, d0, jnp.where(lane == 1, d1, 0))


def _dest(rec, pstart, tm):
    t = rec.shape[0]
    ps = jnp.zeros((1, LANES), F32).at[0, EXPERT_LANE0:EXPERT_LANE0 + N_EXPERTS].set(pstart.astype(F32))
    return pl.pallas_call(
        _dest_kernel,
        grid=(t // tm,),
        in_specs=[pl.BlockSpec((tm, LANES), lambda i: (i, 0)), pl.BlockSpec((1, LANES), lambda i: (0, 0))],
        out_specs=pl.BlockSpec((tm, LANES), lambda i: (i, 0)),
        out_shape=jax.ShapeDtypeStruct((t, LANES), I32),
        compiler_params=_cparams(("parallel",)),
        name="dest",
    )(rec, ps)


def _dispatch_kernel(ps_ref, cnt_ref, nu_ref, d_ref, h_ref, xs_ref, zero_scr, sem, zsem):
    tm = h_ref.shape[0]
    bm = zero_scr.shape[0]

    def zero_fill(wait):
        def expert(e, carry):
            first = ps_ref[e] + cnt_ref[e]
            n_pad = (bm - (cnt_ref[e] & (bm - 1))) & (bm - 1)

            def row(r, c):
                copy = _row_copy(zero_scr, 0, xs_ref, first + r, zsem)
                copy.wait() if wait else copy.start()
                return c

            lax.fori_loop(0, n_pad, row, 0)
            return carry

        lax.fori_loop(0, cnt_ref.shape[0], expert, 0)

        def tail(blk, carry):
            copy = pltpu.make_async_copy(zero_scr, xs_ref.at[pl.ds(blk * bm, bm)], zsem)
            copy.wait() if wait else copy.start()
            return carry

        lax.fori_loop(nu_ref[0], xs_ref.shape[0] // bm, tail, 0)

    @pl.when(pl.program_id(0) == 0)
    def _():
        zero_scr[...] = jnp.zeros_like(zero_scr)
        zero_fill(wait=False)

    def start(j, carry):
        for u in range(DMA_UNROLL):
            tok = j * DMA_UNROLL + u
            for k in range(2):
                _row_copy(h_ref, tok, xs_ref, d_ref[2 * tok + k], sem).start(priority=k)
        return carry

    lax.fori_loop(0, tm // DMA_UNROLL, start, 0)

    def wait(j, carry):
        for _ in range(2 * DMA_UNROLL):
            _row_copy(h_ref, 0, xs_ref, 0, sem).wait()
        return carry

    lax.fori_loop(0, tm // DMA_UNROLL, wait, 0)

    @pl.when(pl.program_id(0) == pl.num_programs(0) - 1)
    def _():
        zero_fill(wait=True)


def _dispatch(h2, dest, pstart, counts, n_used, cap, tm):
    t, d = h2.shape
    return pl.pallas_call(
        _dispatch_kernel,
        grid_spec=pltpu.PrefetchScalarGridSpec(
            num_scalar_prefetch=3, grid=(t // tm,),
            in_specs=[pl.BlockSpec((2 * tm,), lambda i, ps, cnt, nu: (i,), memory_space=pltpu.SMEM),
                      pl.BlockSpec((tm, d), lambda i, ps, cnt, nu: (i, 0))],
            out_specs=pl.BlockSpec(memory_space=pl.ANY),
            scratch_shapes=[pltpu.VMEM((MOE_ROWS, d), h2.dtype), pltpu.SemaphoreType.DMA(()),
                            pltpu.SemaphoreType.DMA(())]),
        out_shape=jax.ShapeDtypeStruct((cap, d), h2.dtype),
        compiler_params=_cparams(("arbitrary",)),
        name="dispatch",
    )(pstart, counts, n_used, dest, h2)


def _experts_kernel(be_ref, nu_ref, seg_ref, nxt_ref, xs_ref, w1_hbm, w3_hbm, w2_hbm, ys_ref,
                    w1f, w3f, w2f, w1b, w3b, w2b, sems):
    i = pl.program_id(0)

    def weight_copies(e, slot):
        return [pltpu.make_async_copy(hbm.at[e], buf.at[slot], sems.at[slot, j])
                for j, (hbm, buf) in enumerate(((w1_hbm, w1f), (w3_hbm, w3f), (w2_hbm, w2f)))]

    @pl.when(i < nu_ref[0])
    def _():
        changed = jnp.logical_or(i == 0, be_ref[i] != be_ref[jnp.maximum(i - 1, 0)])
        slot = seg_ref[i] & 1

        @pl.when(i == 0)
        def _():
            for copy in weight_copies(be_ref[0], 0):
                copy.start()

        @pl.when(changed)
        def _():
            for copy in weight_copies(be_ref[i], slot):
                copy.wait()
            w1b[...] = w1f[slot].astype(BF16)
            w3b[...] = w3f[slot].astype(BF16)
            w2b[...] = w2f[slot].astype(BF16)

            @pl.when(nxt_ref[i] >= 0)
            def _():
                for copy in weight_copies(nxt_ref[i], 1 - slot):
                    copy.start()

        grp = xs_ref.shape[0] // MOE_ROW_GROUPS
        rows = lambda g: slice(g * grp, (g + 1) * grp)

        def up(g):
            x = xs_ref[rows(g), :].astype(BF16)
            return _dot(x, w1b[...]), _dot(x, w3b[...])

        def down(g, h1, h3):
            hid = h1 * jax.nn.sigmoid(h1) * h3
            ys_ref[rows(g), :] = _dot(hid.astype(BF16), w2b[...])

        nxt = up(0)
        for g in range(MOE_ROW_GROUPS):
            cur = nxt
            if g + 1 < MOE_ROW_GROUPS:
                nxt = up(g + 1)
            down(g, *cur)

    @pl.when(i >= nu_ref[0])
    def _():
        ys_ref[...] = jnp.zeros_like(ys_ref)


def _experts(xs, w1, w3, w2, block_expert, n_used, segment, next_expert):
    cap, d = xs.shape
    bm = MOE_ROWS
    ff = w1.shape[2]
    hbm = pl.BlockSpec(memory_space=pl.ANY)
    return pl.pallas_call(
        _experts_kernel,
        grid_spec=pltpu.PrefetchScalarGridSpec(
            num_scalar_prefetch=4, grid=(cap // bm,),
            in_specs=[pl.BlockSpec((bm, d), lambda i, be, nu, seg, nxt: (jnp.minimum(i, nu[0] - 1), 0)),
                      hbm, hbm, hbm],
            out_specs=pl.BlockSpec((bm, d), lambda i, be, nu, seg, nxt: (i, 0)),
            scratch_shapes=[pltpu.VMEM((2, d, ff), F32), pltpu.VMEM((2, d, ff), F32), pltpu.VMEM((2, ff, d), F32),
                            pltpu.VMEM((d, ff), BF16), pltpu.VMEM((d, ff), BF16), pltpu.VMEM((ff, d), BF16),
                            pltpu.SemaphoreType.DMA((2, 3))]),
        out_shape=jax.ShapeDtypeStruct((cap, d), F32),
        compiler_params=_cparams(("arbitrary",)),
        name="experts",
    )(block_expert, n_used, segment, next_expert, xs, w1, w3, w2)


def _final_kernel(d_ref, x1_ref, rec_ref, nw_ref, ys_ref, o_ref, buf, sem):
    tm = x1_ref.shape[0]

    def start(j, carry):
        for u in range(DMA_UNROLL):
            tok = j * DMA_UNROLL + u
            for k in range(2):
                _row_copy(ys_ref, d_ref[2 * tok + k], buf.at[k], tok, sem).start(priority=k)
        return carry

    lax.fori_loop(0, tm // DMA_UNROLL, start, 0)

    def wait(j, carry):
        for _ in range(2 * DMA_UNROLL):
            _row_copy(ys_ref, 0, buf.at[0], 0, sem).wait()
        return carry

    lax.fori_loop(0, tm // DMA_UNROLL, wait, 0)

    rec = rec_ref[...]
    x = x1_ref[...] + buf[0] * rec[:, _R_W0:_R_W0 + 1] + buf[1] * rec[:, _R_W1:_R_W1 + 1]
    o_ref[...] = x * lax.rsqrt(jnp.mean(x * x, axis=-1, keepdims=True) + EPS) * nw_ref[...]


def _final(x1, rec, ys, dest, final_norm_w, tm):
    t, d = x1.shape
    nw = final_norm_w.reshape(1, d)
    return pl.pallas_call(
        _final_kernel,
        grid=(t // tm,),
        in_specs=[pl.BlockSpec((2 * tm,), lambda i: (i,), memory_space=pltpu.SMEM),
                  pl.BlockSpec((tm, d), lambda i: (i, 0)),
                  pl.BlockSpec((tm, LANES), lambda i: (i, 0)),
                  pl.BlockSpec((1, d), lambda i: (0, 0)),
                  pl.BlockSpec(memory_space=pl.ANY)],
        out_specs=pl.BlockSpec((tm, d), lambda i: (i, 0)),
        out_shape=jax.ShapeDtypeStruct((t, d), F32),
        scratch_shapes=[pltpu.VMEM((2, tm, d), F32), pltpu.SemaphoreType.DMA(())],
        compiler_params=_cparams(("arbitrary",)),
        name="final",
    )(dest, x1, rec, nw, ys)


def _moe_layout(counts):
    bm = MOE_ROWS
    padded = (counts + bm - 1) // bm * bm
    pends = jnp.cumsum(padded)
    pstart = pends - padded
    return pstart.astype(I32), pends.astype(I32)


def kernel(x, positions, attn_norm_w, w_in, hg_lower_bound, hg_out_norm_w, mla_q_norm_w, mla_w_uq, mla_kv_norm_w, mla_w_ukv, w_branch_hgrn, w_branch_mla, w_out, ffn_norm_w, router_group_w, router_group_b, router_expert_w, router_expert_b, expert_w1, expert_w3, expert_w2, final_norm_w):
    batch, seq, d = x.shape
    t = batch * seq
    depth = w_in.shape[0]
    assert d == 1024 and seq % 512 == 0 and t % 1024 == 0
    x2 = x.reshape(t, d)
    for l in range(depth):
        q, kk, lf, v, g, cq, ckv, kr, sa, sb = _in_proj(x2, attn_norm_w[l], w_in[l], hg_lower_bound, l, tm=256)
        o_a = _hgrn(q, kk, lf, v, g, hg_out_norm_w[l], batch, seq)
        mq, mk, mv = _mla_prep(cq, ckv, kr, positions, mla_q_norm_w[l], mla_w_uq[l], mla_kv_norm_w[l],
                               mla_w_ukv[l], batch, seq, tm=512)
        o_b = _flash(mq, mk, mv, tq=256, heads_per_step=2).reshape(t, MLA_HEADS * MLA_VDIM)
        x1, h2, rec, cnt = _combine(o_a, o_b, sa, sb, x2, w_branch_hgrn[l], w_branch_mla[l], w_out[l],
                                    ffn_norm_w[l], router_group_w[l], router_group_b[l],
                                    router_expert_w[l], router_expert_b[l], tm=1024)

        counts = cnt[0, EXPERT_LANE0:EXPERT_LANE0 + N_EXPERTS].astype(I32)
        pstart, pends = _moe_layout(counts)
        n_blocks = (2 * t) // MOE_ROWS + N_EXPERTS
        cap = n_blocks * MOE_ROWS
        n_used = (pends[-1] // MOE_ROWS).reshape(1)
        blk = jnp.minimum(jnp.arange(n_blocks, dtype=I32), n_used[0] - 1)
        be = jnp.sum((blk[:, None] * MOE_ROWS >= pends[None, :]).astype(I32), axis=1)
        eid = jnp.arange(N_EXPERTS, dtype=I32)
        nonempty = counts > 0
        ordinal = jnp.cumsum(nonempty.astype(I32)) - 1
        later = nonempty[None, :] & (eid[None, :] > eid[:, None])
        nxt_e = jnp.min(jnp.where(later, eid[None, :], N_EXPERTS), axis=1)
        nxt_e = jnp.where(nxt_e == N_EXPERTS, -1, nxt_e)
        onehot = (be[:, None] == eid[None, :]).astype(I32)
        segment = jnp.sum(onehot * ordinal[None, :], axis=1)
        next_expert = jnp.sum(onehot * nxt_e[None, :], axis=1)
        dest = _dest(rec, pstart, tm=math.gcd(t, 2048))[:, 0:2].reshape(2 * t)

        xs = _dispatch(h2, dest, pstart, counts, n_used, cap, tm=512)
        ys = _experts(xs, expert_w1[l], expert_w3[l], expert_w2[l], be, n_used, segment, next_expert)
        assert l == depth - 1, "multi-layer stacking needs an un-normalised residual output"
        out = _final(x1, rec, ys, dest, final_norm_w, tm=512)
    return out.reshape(batch, seq, d)
```

```python
import functools
import math

import numpy as np
import jax
import jax.numpy as jnp
from jax import lax
from jax.experimental import pallas as pl
from jax.experimental.pallas import tpu as pltpu

F32 = jnp.float32
BF16 = jnp.bfloat16
I32 = jnp.int32

EPS = 1e-6
LANES = 128
SUBLANES = 8
HG_HEADS = 8
HG_DIM = 128
MLA_HEADS = 8
MLA_Q_RANK = 384
MLA_KV_RANK = 256
MLA_NOPE = 128
MLA_ROPE = 64
MLA_VDIM = 128
MLA_QK_DIM = MLA_NOPE + MLA_ROPE
MLA_QK_PAD = 256
ROPE_THETA = 10000.0
N_GROUPS = 8
EXPERTS_PER_GROUP = 8
N_EXPERTS = N_GROUPS * EXPERTS_PER_GROUP
EXPERT_LANE0 = N_GROUPS
EXPERT_FF = 512

HG_CHUNK = 128
HG_LOW_BLOCKS = (1, 2, 4)
HG_HIGH_BLOCKS = (8, 16, 32, 64)
HG_HEADS_PER_STEP = 2
HG_CHUNK_UNROLL = 8
COMBINE_SUB = 256
MOE_ROWS = 256
MOE_ROW_GROUPS = 2
DMA_UNROLL = 8
VMEM_LIMIT = 56 * 1024 * 1024


def _cparams(sem):
    return pltpu.CompilerParams(dimension_semantics=sem, vmem_limit_bytes=VMEM_LIMIT)


def _dot(a, b):
    return jnp.dot(a, b, preferred_element_type=F32)


def _dot_nt(a, b):
    return lax.dot_general(a, b, (((1,), (1,)), ((), ())), preferred_element_type=F32)


def _dot_tn(a, b):
    return lax.dot_general(a, b, (((0,), (0,)), ((), ())), preferred_element_type=F32)


def _split_bf16(x):
    hi = x.astype(BF16)
    lo = (x - hi.astype(F32)).astype(BF16)
    return hi, lo


_IN_OFF_Q = 0
_IN_OFF_F = 1024
_IN_OFF_I = 2048
_IN_OFF_G = 3072
_IN_OFF_CQ = 4096
_IN_OFF_CKV = _IN_OFF_CQ + MLA_Q_RANK
_IN_OFF_KR = _IN_OFF_CKV + MLA_KV_RANK
_IN_OFF_GA = _IN_OFF_KR + LANES
_IN_OFF_GB = _IN_OFF_GA + 1024
_IN_COLS_PAD = _IN_OFF_GB + 1024


def _rope_lane_layout(w):
    half = MLA_ROPE // 2
    z = jnp.zeros((w.shape[0], half), w.dtype)
    return jnp.concatenate([w[:, :half], z, w[:, half:], z], axis=1)


def _inproj_kernel(x_ref, nw_ref, lbt_ref, w_ref, q_ref, kk_ref, lf_ref, v_ref, g_ref, cq_ref, ckv_ref,
                   kr_ref, sa_ref, sb_ref, h_scr, *, layer):
    x = x_ref[...]
    ms = jnp.mean(x * x, axis=-1, keepdims=True)
    h_scr[...] = (x * lax.rsqrt(ms + EPS) * nw_ref[...]).astype(BF16)

    tab = lbt_ref[...]
    e = jnp.exp(tab - jnp.max(tab, axis=0, keepdims=True))
    sm = e / jnp.sum(e, axis=0, keepdims=True)
    lb = jnp.sum(sm[:layer + 1], axis=0, keepdims=True)

    def proj(off, width):
        return _dot(h_scr[...], w_ref[:, off:off + width])

    half = 512
    for c in range(0, 1024, half):
        z = proj(_IN_OFF_Q + c, half)
        q_ref[:, c:c + half] = (z * jax.nn.sigmoid(z) * (HG_DIM ** -0.5)).astype(BF16)
        z = proj(_IN_OFF_F + c, half)
        lbc = lb[:, c:c + half]
        fg = lbc + (1.0 - lbc) * jax.nn.sigmoid(z)
        lf_ref[:, c:c + half] = jnp.log(fg)
        kk_ref[:, c:c + half] = (1.0 - fg).astype(BF16)
        v_ref[:, c:c + half] = proj(_IN_OFF_I + c, half).astype(BF16)
        z = proj(_IN_OFF_G + c, half)
        g_ref[:, c:c + half] = (z * jax.nn.sigmoid(z)).astype(BF16)
        sa_ref[:, c:c + half] = jax.nn.sigmoid(proj(_IN_OFF_GA + c, half)).astype(BF16)
        sb_ref[:, c:c + half] = jax.nn.sigmoid(proj(_IN_OFF_GB + c, half)).astype(BF16)
    cq_ref[...] = proj(_IN_OFF_CQ, MLA_Q_RANK)
    ckv_ref[...] = proj(_IN_OFF_CKV, MLA_KV_RANK)
    kr_ref[...] = proj(_IN_OFF_KR, LANES)


def _in_proj(x2, attn_norm_w, w_in, lb_table, layer, tm):
    t, d = x2.shape
    parts = []
    off = 0
    for width in (1024, 1024, 1024, 1024, MLA_Q_RANK, MLA_KV_RANK, MLA_ROPE, 1024, 1024):
        parts.append(w_in[:, off:off + width])
        off += width
    parts[6] = _rope_lane_layout(parts[6])
    w = jnp.concatenate([p.astype(BF16) for p in parts], axis=1)
    assert w.shape[1] == _IN_COLS_PAD
    row = lambda width: pl.BlockSpec((tm, width), lambda i: (i, 0))
    full = lambda a: pl.BlockSpec(a.shape, lambda i: (0,) * a.ndim)
    nw = attn_norm_w.reshape(1, d)
    out_shape = [jax.ShapeDtypeStruct((t, 1024), BF16),
                 jax.ShapeDtypeStruct((t, 1024), BF16),
                 jax.ShapeDtypeStruct((t, 1024), F32),
                 jax.ShapeDtypeStruct((t, 1024), BF16),
                 jax.ShapeDtypeStruct((t, 1024), BF16),
                 jax.ShapeDtypeStruct((t, MLA_Q_RANK), F32),
                 jax.ShapeDtypeStruct((t, MLA_KV_RANK), F32),
                 jax.ShapeDtypeStruct((t, LANES), F32),
                 jax.ShapeDtypeStruct((t, 1024), BF16),
                 jax.ShapeDtypeStruct((t, 1024), BF16)]
    return pl.pallas_call(
        functools.partial(_inproj_kernel, layer=layer),
        grid=(t // tm,),
        in_specs=[row(d), full(nw), full(lb_table),
                  pl.BlockSpec(w.shape, lambda i: (0, 0), pipeline_mode=pl.Buffered(1))],
        out_specs=[row(s.shape[1]) for s in out_shape],
        out_shape=out_shape,
        scratch_shapes=[pltpu.VMEM((tm, d), BF16)],
        compiler_params=_cparams(("parallel",)),
        name="in_proj",
    )(x2, nw, lb_table, w)


def _hgrn_tables():
    n = HG_CHUNK
    r = np.arange(n)[:, None]
    j = np.arange(n)[None, :]
    mats = [j <= r]
    for m in HG_LOW_BLOCKS:
        c = (r // (2 * m)) * (2 * m) + m - 1
        odd = (r & m) != 0
        mats.append(np.where(odd, (j > c) & (j <= r), (j > r) & (j <= c)))
    tab = np.concatenate(mats, axis=0).astype(np.float32)
    return jnp.asarray(np.concatenate([tab, tab], axis=1), dtype=BF16)


def _hgrn_kernel(q_ref, kk_ref, lf_ref, v_ref, g_ref, nw_ref, tab_ref, o_ref, st_ref, *, heads, n_chunks):
    n = HG_CHUNK
    unroll = math.gcd(n_chunks, HG_CHUNK_UNROLL)
    assert n == HG_DIM
    row = lax.broadcasted_iota(I32, (n, n), 0)
    col = lax.broadcasted_iota(I32, (n, n), 1)
    code = jnp.where(col <= row, 32 - lax.clz(row ^ col), -1)
    st_ref[...] = jnp.zeros_like(st_ref)
    nw = nw_ref[...]

    def load(r0, h):
        ls = slice(h * HG_DIM, (h + 1) * HG_DIM)
        qb = q_ref[pl.ds(r0, n), ls]
        kb = kk_ref[pl.ds(r0, n), ls]
        hi, lo = _split_bf16(lf_ref[pl.ds(r0, n), ls])
        ex = _dot(tab_ref[...], jnp.concatenate([hi, lo], axis=0))
        return dict(r0=r0, h=h, ls=ls, qb=qb, kb=kb, vb=v_ref[pl.ds(r0, n), ls], ex=ex)

    def low_products(p):
        q, k, ex = p["qb"].astype(F32), p["kb"].astype(F32), p["ex"]
        prods = [_dot_nt(p["qb"], p["kb"])]
        for i, m in enumerate(HG_LOW_BLOCKS):
            odd = (row & m) != 0
            w = jnp.where(odd, q, k) * jnp.exp(ex[(i + 1) * n:(i + 2) * n, :])
            prods.append(_dot_nt(jnp.where(odd, w, 0.0).astype(BF16), jnp.where(odd, 0.0, w).astype(BF16)))
        p["low"] = prods

    def high_products(p):
        q, k, b = p["qb"].astype(F32), p["kb"].astype(F32), p["ex"][0:n, :]
        prods = []
        for m in HG_HIGH_BLOCKS:
            bases = range(0, n, 2 * m)
            e_parts, s_parts = [], []
            for base in bases:
                mid = b[base + m - 1:base + m, :]
                e_parts += [mid - b[base:base + m, :], b[base + m:base + 2 * m, :] - mid]
                s_parts += [k[base:base + m, :], q[base + m:base + 2 * m, :]]
            w = jnp.concatenate(s_parts, axis=0) * jnp.exp(jnp.concatenate(e_parts, axis=0))
            zero = jnp.zeros((m, HG_DIM), F32)
            kp = jnp.concatenate([x for base in bases for x in (w[base:base + m, :], zero)], axis=0)
            qc = jnp.concatenate([w[base + m:base + 2 * m, :] for base in bases], axis=0)
            prods.append(_dot_nt(qc.astype(BF16), kp.astype(BF16)))
        p["high"] = prods

    def intra(p):
        a = jnp.where(code == 0, p["low"][0], 0.0)
        for m, r in zip(HG_LOW_BLOCKS, p["low"][1:]):
            a = jnp.where(code == m.bit_length(), r, a)
        for m, r in zip(HG_HIGH_BLOCKS, p["high"]):
            pieces = []
            for j, base in enumerate(range(0, n, 2 * m)):
                second = slice(base + m, base + 2 * m)
                pieces += [a[base:base + m, :],
                           jnp.where(code[second, :] == m.bit_length(), r[j * m:(j + 1) * m, :], a[second, :])]
            a = jnp.concatenate(pieces, axis=0)
        q, k, b = p["qb"].astype(F32), p["kb"].astype(F32), p["ex"][0:n, :]
        b_last = b[n - 1:n, :]
        p["o"] = _dot(a.astype(BF16), p["vb"])
        p["qe"] = (q * jnp.exp(b)).astype(BF16)
        p["upd"] = _dot_tn(p["vb"], (k * jnp.exp(b_last - b)).astype(BF16))
        p["decay"] = jnp.exp(b_last)

    def finish(p, st):
        o = p["o"] + _dot_nt(p["qe"], st.astype(BF16))
        ms = jnp.mean(o * o, axis=-1, keepdims=True)
        y = o * lax.rsqrt(ms + EPS) * nw
        o_ref[pl.ds(p["r0"], n), p["ls"]] = (y * g_ref[pl.ds(p["r0"], n), p["ls"]].astype(F32)).astype(BF16)
        return st * p["decay"] + p["upd"]

    stages = (low_products, high_products, intra)

    def chunk_group(cg, carry):
        pairs = [(u, h) for u in range(unroll) for h in range(heads)]
        state = [st_ref[h] for h in range(heads)]
        live = {}
        for step in range(len(pairs) + len(stages) + 1):
            if step < len(pairs):
                u, h = pairs[step]
                live[step] = load(pl.multiple_of((cg * unroll + u) * n, n), h)
            for d, stage in enumerate(stages, start=1):
                if 0 <= step - d < len(pairs):
                    stage(live[step - d])
            done = step - len(stages) - 1
            if 0 <= done < len(pairs):
                h = pairs[done][1]
                state[h] = finish(live.pop(done), state[h])
        for h in range(heads):
            st_ref[h] = state[h]
        return carry

    lax.fori_loop(0, n_chunks // unroll, chunk_group, 0)


def _hgrn(q, kk, lf, v, g, out_norm_w, batch, seq):
    t, width = q.shape
    hp = HG_HEADS_PER_STEP
    wblk = hp * HG_DIM
    tab = _hgrn_tables()
    spec = pl.BlockSpec((seq, wblk), lambda b, j: (b, j))
    full = lambda a: pl.BlockSpec(a.shape, lambda b, j: (0,) * a.ndim)
    nw = out_norm_w.reshape(1, HG_DIM)
    return pl.pallas_call(
        functools.partial(_hgrn_kernel, heads=hp, n_chunks=seq // HG_CHUNK),
        grid=(batch, width // wblk),
        in_specs=[spec, spec, spec, spec, spec, full(nw), full(tab)],
        out_specs=spec,
        out_shape=jax.ShapeDtypeStruct((t, width), BF16),
        scratch_shapes=[pltpu.VMEM((hp, HG_DIM, HG_DIM), F32)],
        compiler_params=_cparams(("parallel", "parallel")),
        name="hgrn",
    )(q, kk, lf, v, g, nw, tab)


def _rope(p, cos_t, sin_t):
    return p * cos_t + pltpu.roll(p, LANES // 2, axis=1) * sin_t


def _rope_tables(pos, freq):
    rows = pos.shape[0]
    qr = rows // 4
    half = MLA_ROPE // 2
    lane = lax.broadcasted_iota(I32, (qr, LANES), 1)
    packed = jnp.where(lane < half, pos[0:qr], jnp.where(lane < 2 * half, pos[qr:2 * qr],
                       jnp.where(lane < 3 * half, pos[2 * qr:3 * qr], pos[3 * qr:])))
    ang = packed * freq
    cos_p, sin_p = jnp.cos(ang), jnp.sin(ang)
    first = lane < half
    second = (lane >= 2 * half) & (lane < 3 * half)
    roll = lambda x, shift: x if shift % LANES == 0 else pltpu.roll(x, shift % LANES, axis=1)
    cos_parts, sin_parts = [], []
    for j in range(4):
        to_first, to_second = -half * j, 2 * half - half * j
        cos_parts.append(jnp.where(first, roll(cos_p, to_first), jnp.where(second, roll(cos_p, to_second), 0.0)))
        sin_parts.append(jnp.where(first, -roll(sin_p, to_first), jnp.where(second, roll(sin_p, to_second), 0.0)))
    return jnp.concatenate(cos_parts, axis=0), jnp.concatenate(sin_parts, axis=0)


def _mla_prep_kernel(cq_ref, ckv_ref, kr_ref, pos_ref, qnw_ref, kvnw_ref, wq_ref, wkv_ref, rot_ref,
                     q_ref, k_ref, v_ref):
    cos_t, sin_t = _rope_tables(pos_ref[...], rot_ref[...])

    cq = cq_ref[...]
    cqn = cq * lax.rsqrt(jnp.mean(cq * cq, axis=-1, keepdims=True) + EPS) * qnw_ref[...]
    ckv = ckv_ref[...]
    ckvn = ckv * lax.rsqrt(jnp.mean(ckv * ckv, axis=-1, keepdims=True) + EPS) * kvnw_ref[...]
    cqn = cqn.astype(BF16)
    ckvn = ckvn.astype(BF16)
    k_pe = _rope(kr_ref[...], cos_t, sin_t).astype(BF16)
    scale = MLA_QK_DIM ** -0.5
    for h in range(MLA_HEADS):
        c0 = h * MLA_QK_PAD
        qh = _dot(cqn, wq_ref[:, c0:c0 + MLA_QK_PAD])
        q_ref[h, :, 0:MLA_NOPE] = (qh[:, :MLA_NOPE] * scale).astype(BF16)
        q_ref[h, :, MLA_NOPE:MLA_QK_PAD] = (_rope(qh[:, MLA_NOPE:], cos_t, sin_t) * scale).astype(BF16)
        kvh = _dot(ckvn, wkv_ref[:, c0:c0 + MLA_NOPE + MLA_VDIM])
        k_ref[h, :, 0:MLA_NOPE] = kvh[:, :MLA_NOPE].astype(BF16)
        k_ref[h, :, MLA_NOPE:MLA_QK_PAD] = k_pe
        v_ref[h] = kvh[:, MLA_NOPE:].astype(BF16)


def _mla_prep(cq, ckv, kr, positions, q_norm_w, w_uq, kv_norm_w, w_ukv, batch, seq, tm):
    t = cq.shape[0]
    half = MLA_ROPE // 2
    wq = w_uq.reshape(MLA_Q_RANK, MLA_HEADS, MLA_QK_DIM)
    zq = jnp.zeros((MLA_Q_RANK, MLA_HEADS, half), w_uq.dtype)
    wq = jnp.concatenate([wq[:, :, :MLA_NOPE], wq[:, :, MLA_NOPE:MLA_NOPE + half], zq,
                          wq[:, :, MLA_NOPE + half:], zq], axis=2)
    wq = wq.reshape(MLA_Q_RANK, MLA_HEADS * MLA_QK_PAD).astype(BF16)
    wkv = w_ukv.astype(BF16)
    inv_freq = ROPE_THETA ** (-jnp.arange(half, dtype=F32) / half)
    rot = jnp.tile(inv_freq, LANES // half).reshape(1, LANES)
    pos = positions.astype(F32).reshape(t, 1)
    nblk = seq // tm
    row = lambda width: pl.BlockSpec((tm, width), lambda i: (i, 0))
    full = lambda a: pl.BlockSpec(a.shape, lambda i: (0,) * a.ndim)
    hspec = lambda width: pl.BlockSpec((None, MLA_HEADS, tm, width), lambda i: (i // nblk, 0, i % nblk, 0))
    qnw = q_norm_w.reshape(1, -1)
    kvnw = kv_norm_w.reshape(1, -1)
    return pl.pallas_call(
        _mla_prep_kernel,
        grid=(t // tm,),
        in_specs=[row(MLA_Q_RANK), row(MLA_KV_RANK), row(LANES), row(1), full(qnw), full(kvnw),
                  full(wq), full(wkv), full(rot)],
        out_specs=[hspec(MLA_QK_PAD), hspec(MLA_QK_PAD), hspec(MLA_VDIM)],
        out_shape=[jax.ShapeDtypeStruct((batch, MLA_HEADS, seq, MLA_QK_PAD), BF16),
                   jax.ShapeDtypeStruct((batch, MLA_HEADS, seq, MLA_QK_PAD), BF16),
                   jax.ShapeDtypeStruct((batch, MLA_HEADS, seq, MLA_VDIM), BF16)],
        compiler_params=_cparams(("parallel",)),
        name="mla_prep",
    )(cq, ckv, kr, pos, qnw, kvnw, wq, wkv, rot)


def _flash_kernel(q_ref, k_ref, v_ref, o_ref, v1_scr, *, tq):
    heads, seq = q_ref.shape[0], q_ref.shape[1]
    row = lax.broadcasted_iota(I32, (tq, tq), 0)
    col = lax.broadcasted_iota(I32, (tq, tq), 1)
    causal = row >= col
    items = [(h, i) for h in range(heads) for i in range(seq // tq)]

    def scores(h, i):
        q = q_ref[h, i * tq:(i + 1) * tq, :]
        return _dot_nt(q, k_ref[h, 0:(i + 1) * tq, :])

    def softmax(h, i, s):
        lo = i * tq
        sd = jnp.where(causal, s[:, lo:lo + tq], -jnp.inf)
        m = jnp.max(sd, axis=-1, keepdims=True)
        if i > 0:
            m = jnp.maximum(m, jnp.max(s[:, 0:lo], axis=-1, keepdims=True))
        p = jnp.exp(jnp.concatenate([s[:, 0:lo], sd], axis=1) - m) if i > 0 else jnp.exp(sd - m)
        return p.astype(BF16)

    v1_scr[:, :, 0:MLA_VDIM] = v_ref[...]
    v1_scr[:, :, MLA_VDIM:] = jnp.ones((heads, seq, MLA_VDIM), BF16)

    def output(h, i, p):
        hi = (i + 1) * tq
        acc = _dot(p, v1_scr[h, 0:hi, :])
        o_ref[i * tq:hi, h * MLA_VDIM:(h + 1) * MLA_VDIM] = (acc[:, 0:MLA_VDIM] / acc[:, MLA_VDIM:]).astype(BF16)

    s = {n: scores(*items[n]) for n in range(min(2, len(items)))}
    pending = softmax(*items[0], s.pop(0))
    for n in range(len(items)):
        if n + 2 < len(items):
            s[n + 2] = scores(*items[n + 2])
        output(*items[n], pending)
        if n + 1 < len(items):
            pending = softmax(*items[n + 1], s.pop(n + 1))


def _flash(q, k, v, tq, heads_per_step):
    batch, heads, seq, _ = q.shape
    hp = heads_per_step
    qk = pl.BlockSpec((None, hp, seq, MLA_QK_PAD), lambda b, h: (b, h, 0, 0))
    return pl.pallas_call(
        functools.partial(_flash_kernel, tq=tq),
        grid=(batch, heads // hp),
        in_specs=[qk, qk, pl.BlockSpec((None, hp, seq, MLA_VDIM), lambda b, h: (b, h, 0, 0))],
        out_specs=pl.BlockSpec((None, seq, hp * MLA_VDIM), lambda b, h: (b, 0, h)),
        out_shape=jax.ShapeDtypeStruct((batch, seq, heads * MLA_VDIM), BF16),
        scratch_shapes=[pltpu.VMEM((hp, seq, 2 * MLA_VDIM), BF16)],
        compiler_params=_cparams(("parallel", "parallel")),
        name="flash",
    )(q, k, v)


_R_E0, _R_E1, _R_W0, _R_W1, _R_RANK0, _R_RANK1 = 0, 1, 2, 3, 4, 5


def _first_lane_equal(x, value, lane):
    return jnp.min(jnp.where(x == value, lane, LANES), axis=-1, keepdims=True)


def _combine_kernel(oa_ref, ob_ref, sa_ref, sb_ref, x_ref, wa_ref, wb_ref, wo_ref, fnw_ref, wr_ref, br_ref,
                    lt_ref, x1_ref, h2_ref, rec_ref, cnt_ref, run_ref, y_scr):
    i = pl.program_id(0)

    @pl.when(i == 0)
    def _():
        run_ref[...] = jnp.zeros_like(run_ref)

    sub = lt_ref.shape[0]
    n_sub = x_ref.shape[0] // sub
    half = x_ref.shape[1] // 2
    rows = lambda s: slice(s * sub, (s + 1) * sub)

    def branches(s):
        for c in (0, half):
            cols = slice(c, c + half)
            ya = _dot(oa_ref[rows(s), :], wa_ref[:, cols])
            yb = _dot(ob_ref[rows(s), :], wb_ref[:, cols])
            y = sa_ref[rows(s), cols].astype(F32) * ya + sb_ref[rows(s), cols].astype(F32) * yb
            y_scr[rows(s), cols] = y.astype(BF16)

    def residual(s):
        for c in (0, half):
            cols = slice(c, c + half)
            x1_ref[rows(s), cols] = x_ref[rows(s), cols] + _dot(y_scr[rows(s), :], wo_ref[:, cols])

    def router(s):
        x1 = x1_ref[rows(s), :]
        h2 = x1 * lax.rsqrt(jnp.mean(x1 * x1, axis=-1, keepdims=True) + EPS) * fnw_ref[...]
        h2_ref[rows(s), :] = h2
        hh, hl = _split_bf16(h2)
        both = _dot(hh, wr_ref[...])
        return both[:, 0:LANES] + both[:, LANES:] + _dot(hl, wr_ref[:, 0:LANES]) + br_ref[...]

    run = run_ref[...]
    logits = {}
    for step in range(n_sub + 3):
        if step < n_sub:
            branches(step)
        if 0 <= step - 1 < n_sub:
            residual(step - 1)
        if 0 <= step - 2 < n_sub:
            logits[step - 2] = router(step - 2)
        if 0 <= step - 3 < n_sub:
            rec, run = _route(logits.pop(step - 3), run, lt_ref[...])
            rec_ref[rows(step - 3), :] = rec
    run_ref[...] = run
    cnt_ref[...] = run


def _route(logits, run, lt):
    tm = logits.shape[0]
    lane = lax.broadcasted_iota(I32, (tm, LANES), 1)
    neg = -jnp.inf

    gl = jnp.where(lane < N_GROUPS, logits, neg)
    ge = jnp.exp(gl - jnp.max(gl, axis=-1, keepdims=True))
    gp = ge / jnp.sum(ge, axis=-1, keepdims=True)
    g_w = jnp.max(gp, axis=-1, keepdims=True)
    g_idx = _first_lane_equal(gp, g_w, lane)

    ej = lane - EXPERT_LANE0
    sel = (ej >= 0) & (ej < N_EXPERTS) & ((ej >> 3) == g_idx)
    el = jnp.where(sel, logits, neg)
    ee = jnp.exp(el - jnp.max(el, axis=-1, keepdims=True))
    ep = jnp.where(sel, ee / jnp.sum(ee, axis=-1, keepdims=True), -1.0)
    p0 = jnp.max(ep, axis=-1, keepdims=True)
    l0 = _first_lane_equal(ep, p0, lane)
    ep1 = jnp.where(lane == l0, -1.0, ep)
    p1 = jnp.max(ep1, axis=-1, keepdims=True)
    l1 = _first_lane_equal(ep1, p1, lane)
    psum = p0 + p1
    w0 = g_w * (p0 / psum)
    w1 = g_w * (p1 / psum)

    oh0 = lane == l0
    oh1 = lane == l1
    oh = jnp.where(oh0 | oh1, 1.0, 0.0)
    before = run + _dot(lt, oh.astype(BF16))
    rank0 = jnp.sum(jnp.where(oh0, before, 0.0), axis=-1, keepdims=True)
    rank1 = jnp.sum(jnp.where(oh1, before, 0.0), axis=-1, keepdims=True)
    run = run + jnp.sum(oh, axis=0, keepdims=True)

    rec = jnp.where(lane == _R_E0, (l0 - EXPERT_LANE0).astype(F32), 0.0)
    rec = jnp.where(lane == _R_E1, (l1 - EXPERT_LANE0).astype(F32), rec)
    rec = jnp.where(lane == _R_W0, w0, rec)
    rec = jnp.where(lane == _R_W1, w1, rec)
    rec = jnp.where(lane == _R_RANK0, rank0, rec)
    rec = jnp.where(lane == _R_RANK1, rank1, rec)
    return rec, run


def _combine(o_a, o_b, sa, sb, x2, w_a, w_b, w_o, ffn_norm_w, w_group, b_group, w_expert, b_expert, tm):
    t, d = x2.shape
    pad = LANES - N_GROUPS - N_EXPERTS
    wr = jnp.concatenate([w_group, w_expert, jnp.zeros((d, pad), F32)], axis=1)
    wr_hi = wr.astype(BF16)
    wr_lo = (wr - wr_hi.astype(F32)).astype(BF16)
    wr2 = jnp.concatenate([wr_hi, wr_lo], axis=1)
    br = jnp.concatenate([b_group, b_expert, jnp.zeros((pad,), F32)]).reshape(1, LANES)
    lt = jnp.asarray(np.tril(np.ones((COMBINE_SUB, COMBINE_SUB), np.float32), -1), dtype=BF16)
    fnw = ffn_norm_w.reshape(1, d)
    row = lambda width: pl.BlockSpec((tm, width), lambda i: (i, 0))
    full = lambda a: pl.BlockSpec(a.shape, lambda i: (0,) * a.ndim)
    wa, wb, wo = w_a.astype(BF16), w_b.astype(BF16), w_o.astype(BF16)
    return pl.pallas_call(
        _combine_kernel,
        grid=(t // tm,),
        in_specs=[row(d), row(d), row(d), row(d), row(d), full(wa), full(wb), full(wo), full(fnw),
                  full(wr2), full(br), full(lt)],
        out_specs=[row(d), row(d), row(LANES), pl.BlockSpec((1, LANES), lambda i: (0, 0))],
        out_shape=[jax.ShapeDtypeStruct((t, d), F32),
                   jax.ShapeDtypeStruct((t, d), F32),
                   jax.ShapeDtypeStruct((t, LANES), F32),
                   jax.ShapeDtypeStruct((1, LANES), F32)],
        scratch_shapes=[pltpu.VMEM((1, LANES), F32), pltpu.VMEM((tm, d), BF16)],
        compiler_params=_cparams(("arbitrary",)),
        name="combine",
    )(o_a, o_b, sa, sb, x2, wa, wb, wo, fnw, wr2, br, lt)


def _row_copy(src_ref, src_row, dst_ref, dst_row, sem):
    return pltpu.make_async_copy(src_ref.at[pl.ds(src_row, 1)], dst_ref.at[pl.ds(dst_row, 1)], sem)


def _dest_kernel(rec_ref, ps_ref, d_ref):
    rec = rec_ref[...]
    lane = lax.broadcasted_iota(I32, rec.shape, 1)
    ps = ps_ref[...]

    def lookup(e_lane, r_lane):
        e = rec[:, e_lane:e_lane + 1].astype(I32) + EXPERT_LANE0
        start = jnp.sum(jnp.where(lane == e, ps, 0.0), axis=-1, keepdims=True)
        return (start + rec[:, r_lane:r_lane + 1]).astype(I32)

    d0 = lookup(_R_E0, _R_RANK0)
    d1 = lookup(_R_E1, _R_RANK1)
    d_ref[...] = jnp.where(lane == 0, d0, jnp.where(lane == 1, d1, 0))


def _dest(rec, pstart, tm):
    t = rec.shape[0]
    ps = jnp.zeros((1, LANES), F32).at[0, EXPERT_LANE0:EXPERT_LANE0 + N_EXPERTS].set(pstart.astype(F32))
    return pl.pallas_call(
        _dest_kernel,
        grid=(t // tm,),
        in_specs=[pl.BlockSpec((tm, LANES), lambda i: (i, 0)), pl.BlockSpec((1, LANES), lambda i: (0, 0))],
        out_specs=pl.BlockSpec((tm, LANES), lambda i: (i, 0)),
        out_shape=jax.ShapeDtypeStruct((t, LANES), I32),
        compiler_params=_cparams(("parallel",)),
        name="dest",
    )(rec, ps)


def _dispatch_kernel(ps_ref, cnt_ref, nu_ref, d_ref, h_ref, xs_ref, zero_scr, sem, zsem):
    tm = h_ref.shape[0]
    bm = zero_scr.shape[0]

    def zero_fill(wait):
        def expert(e, carry):
            first = ps_ref[e] + cnt_ref[e]
            end = first + ((bm - (cnt_ref[e] & (bm - 1))) & (bm - 1))
            tiles = jnp.minimum((first + SUBLANES - 1) & -SUBLANES, end)

            def row(r, c):
                copy = _row_copy(zero_scr, 0, xs_ref, r, zsem)
                copy.wait() if wait else copy.start()
                return c

            def tile(j, c):
                r = pl.multiple_of(tiles + j * SUBLANES, SUBLANES)
                copy = pltpu.make_async_copy(zero_scr.at[pl.ds(0, SUBLANES)], xs_ref.at[pl.ds(r, SUBLANES)], zsem)
                copy.wait() if wait else copy.start()
                return c

            lax.fori_loop(first, tiles, row, 0)
            lax.fori_loop(0, (end - tiles) >> (SUBLANES.bit_length() - 1), tile, 0)
            return carry

        lax.fori_loop(0, cnt_ref.shape[0], expert, 0)

        def tail(blk, carry):
            copy = pltpu.make_async_copy(zero_scr, xs_ref.at[pl.ds(blk * bm, bm)], zsem)
            copy.wait() if wait else copy.start()
            return carry

        lax.fori_loop(nu_ref[0], xs_ref.shape[0] // bm, tail, 0)

    @pl.when(pl.program_id(0) == 0)
    def _():
        zero_scr[...] = jnp.zeros_like(zero_scr)
        zero_fill(wait=False)

    def start(j, carry):
        for u in range(DMA_UNROLL):
            tok = j * DMA_UNROLL + u
            for k in range(2):
                _row_copy(h_ref, tok, xs_ref, d_ref[2 * tok + k], sem).start(priority=k)
        return carry

    lax.fori_loop(0, tm // DMA_UNROLL, start, 0)

    def wait(j, carry):
        for _ in range(2 * DMA_UNROLL):
            _row_copy(h_ref, 0, xs_ref, 0, sem).wait()
        return carry

    lax.fori_loop(0, tm // DMA_UNROLL, wait, 0)

    @pl.when(pl.program_id(0) == pl.num_programs(0) - 1)
    def _():
        zero_fill(wait=True)


def _dispatch(h2, dest, pstart, counts, n_used, cap, tm):
    t, d = h2.shape
    return pl.pallas_call(
        _dispatch_kernel,
        grid_spec=pltpu.PrefetchScalarGridSpec(
            num_scalar_prefetch=3, grid=(t // tm,),
            in_specs=[pl.BlockSpec((2 * tm,), lambda i, ps, cnt, nu: (i,), memory_space=pltpu.SMEM),
                      pl.BlockSpec((tm, d), lambda i, ps, cnt, nu: (i, 0))],
            out_specs=pl.BlockSpec(memory_space=pl.ANY),
            scratch_shapes=[pltpu.VMEM((MOE_ROWS, d), h2.dtype), pltpu.SemaphoreType.DMA(()),
                            pltpu.SemaphoreType.DMA(())]),
        out_shape=jax.ShapeDtypeStruct((cap, d), h2.dtype),
        compiler_params=_cparams(("arbitrary",)),
        name="dispatch",
    )(pstart, counts, n_used, dest, h2)


def _experts_kernel(be_ref, nu_ref, seg_ref, nxt_ref, xs_ref, w1_hbm, w3_hbm, w2_hbm, ys_ref,
                    w1f, w3f, w2f, w1b, w3b, w2b, sems):
    i = pl.program_id(0)

    def weight_copies(e, slot):
        return [pltpu.make_async_copy(hbm.at[e], buf.at[slot], sems.at[slot, j])
                for j, (hbm, buf) in enumerate(((w1_hbm, w1f), (w3_hbm, w3f), (w2_hbm, w2f)))]

    @pl.when(i < nu_ref[0])
    def _():
        changed = jnp.logical_or(i == 0, be_ref[i] != be_ref[jnp.maximum(i - 1, 0)])
        slot = seg_ref[i] & 1

        @pl.when(i == 0)
        def _():
            for copy in weight_copies(be_ref[0], 0):
                copy.start()

        @pl.when(changed)
        def _():
            for copy in weight_copies(be_ref[i], slot):
                copy.wait()
            w1b[...] = w1f[slot].astype(BF16)
            w3b[...] = w3f[slot].astype(BF16)
            w2b[...] = w2f[slot].astype(BF16)

            @pl.when(nxt_ref[i] >= 0)
            def _():
                for copy in weight_copies(nxt_ref[i], 1 - slot):
                    copy.start()

        grp = xs_ref.shape[0] // MOE_ROW_GROUPS
        rows = lambda g: slice(g * grp, (g + 1) * grp)

        def up(g):
            x = xs_ref[rows(g), :].astype(BF16)
            return _dot(x, w1b[...]), _dot(x, w3b[...])

        def down(g, h1, h3):
            hid = h1 * jax.nn.sigmoid(h1) * h3
            ys_ref[rows(g), :] = _dot(hid.astype(BF16), w2b[...])

        nxt = up(0)
        for g in range(MOE_ROW_GROUPS):
            cur = nxt
            if g + 1 < MOE_ROW_GROUPS:
                nxt = up(g + 1)
            down(g, *cur)

    @pl.when(i >= nu_ref[0])
    def _():
        ys_ref[...] = jnp.zeros_like(ys_ref)


def _experts(xs, w1, w3, w2, block_expert, n_used, segment, next_expert):
    cap, d = xs.shape
    bm = MOE_ROWS
    ff = w1.shape[2]
    hbm = pl.BlockSpec(memory_space=pl.ANY)
    return pl.pallas_call(
        _experts_kernel,
        grid_spec=pltpu.PrefetchScalarGridSpec(
            num_scalar_prefetch=4, grid=(cap // bm,),
            in_specs=[pl.BlockSpec((bm, d), lambda i, be, nu, seg, nxt: (jnp.minimum(i, nu[0] - 1), 0)),
                      hbm, hbm, hbm],
            out_specs=pl.BlockSpec((bm, d), lambda i, be, nu, seg, nxt: (i, 0)),
            scratch_shapes=[pltpu.VMEM((2, d, ff), F32), pltpu.VMEM((2, d, ff), F32), pltpu.VMEM((2, ff, d), F32),
                            pltpu.VMEM((d, ff), BF16), pltpu.VMEM((d, ff), BF16), pltpu.VMEM((ff, d), BF16),
                            pltpu.SemaphoreType.DMA((2, 3))]),
        out_shape=jax.ShapeDtypeStruct((cap, d), F32),
        compiler_params=_cparams(("arbitrary",)),
        name="experts",
    )(block_expert, n_used, segment, next_expert, xs, w1, w3, w2)


def _final_kernel(d_ref, dn_ref, x1_ref, rec_ref, nw_ref, ys_ref, o_ref, buf, sems):
    i = pl.program_id(0)
    tm = x1_ref.shape[0]
    slot = i & 1

    def gather(idx_ref, s):
        def start(j, carry):
            for u in range(DMA_UNROLL):
                tok = j * DMA_UNROLL + u
                for k in range(2):
                    _row_copy(ys_ref, idx_ref[2 * tok + k], buf.at[s, k], tok, sems.at[s]).start(priority=k)
            return carry

        lax.fori_loop(0, tm // DMA_UNROLL, start, 0)

    @pl.when(i == 0)
    def _():
        gather(d_ref, 0)

    @pl.when(i + 1 < pl.num_programs(0))
    def _():
        gather(dn_ref, 1 - slot)

    def wait(j, carry):
        for _ in range(2 * DMA_UNROLL):
            _row_copy(ys_ref, 0, buf.at[slot, 0], 0, sems.at[slot]).wait()
        return carry

    lax.fori_loop(0, tm // DMA_UNROLL, wait, 0)

    rec = rec_ref[...]
    x = x1_ref[...] + buf[slot, 0] * rec[:, _R_W0:_R_W0 + 1] + buf[slot, 1] * rec[:, _R_W1:_R_W1 + 1]
    o_ref[...] = x * lax.rsqrt(jnp.mean(x * x, axis=-1, keepdims=True) + EPS) * nw_ref[...]


def _final(x1, rec, ys, dest, final_norm_w, tm):
    t, d = x1.shape
    nw = final_norm_w.reshape(1, d)
    last = t // tm - 1
    return pl.pallas_call(
        _final_kernel,
        grid=(t // tm,),
        in_specs=[pl.BlockSpec((2 * tm,), lambda i: (i,), memory_space=pltpu.SMEM),
                  pl.BlockSpec((2 * tm,), lambda i: (jnp.minimum(i + 1, last),), memory_space=pltpu.SMEM),
                  pl.BlockSpec((tm, d), lambda i: (i, 0)),
                  pl.BlockSpec((tm, LANES), lambda i: (i, 0)),
                  pl.BlockSpec((1, d), lambda i: (0, 0)),
                  pl.BlockSpec(memory_space=pl.ANY)],
        out_specs=pl.BlockSpec((tm, d), lambda i: (i, 0)),
        out_shape=jax.ShapeDtypeStruct((t, d), F32),
        scratch_shapes=[pltpu.VMEM((2, 2, tm, d), F32), pltpu.SemaphoreType.DMA((2,))],
        compiler_params=_cparams(("arbitrary",)),
        name="final",
    )(dest, dest, x1, rec, nw, ys)


def _moe_layout(counts):
    bm = MOE_ROWS
    padded = (counts + bm - 1) // bm * bm
    pends = jnp.cumsum(padded)
    pstart = pends - padded
    return pstart.astype(I32), pends.astype(I32)


def kernel(x, positions, attn_norm_w, w_in, hg_lower_bound, hg_out_norm_w, mla_q_norm_w, mla_w_uq, mla_kv_norm_w, mla_w_ukv, w_branch_hgrn, w_branch_mla, w_out, ffn_norm_w, router_group_w, router_group_b, router_expert_w, router_expert_b, expert_w1, expert_w3, expert_w2, final_norm_w):
    batch, seq, d = x.shape
    t = batch * seq
    depth = w_in.shape[0]
    assert d == 1024 and seq % 512 == 0 and t % 1024 == 0
    x2 = x.reshape(t, d)
    for l in range(depth):
        q, kk, lf, v, g, cq, ckv, kr, sa, sb = _in_proj(x2, attn_norm_w[l], w_in[l], hg_lower_bound, l, tm=256)
        o_a = _hgrn(q, kk, lf, v, g, hg_out_norm_w[l], batch, seq)
        mq, mk, mv = _mla_prep(cq, ckv, kr, positions, mla_q_norm_w[l], mla_w_uq[l], mla_kv_norm_w[l],
                               mla_w_ukv[l], batch, seq, tm=512)
        o_b = _flash(mq, mk, mv, tq=256, heads_per_step=2).reshape(t, MLA_HEADS * MLA_VDIM)
        x1, h2, rec, cnt = _combine(o_a, o_b, sa, sb, x2, w_branch_hgrn[l], w_branch_mla[l], w_out[l],
                                    ffn_norm_w[l], router_group_w[l], router_group_b[l],
                                    router_expert_w[l], router_expert_b[l], tm=1024)

        counts = cnt[0, EXPERT_LANE0:EXPERT_LANE0 + N_EXPERTS].astype(I32)
        pstart, pends = _moe_layout(counts)
        n_blocks = (2 * t) // MOE_ROWS + N_EXPERTS
        cap = n_blocks * MOE_ROWS
        n_used = (pends[-1] // MOE_ROWS).reshape(1)
        blk = jnp.minimum(jnp.arange(n_blocks, dtype=I32), n_used[0] - 1)
        be = jnp.sum((blk[:, None] * MOE_ROWS >= pends[None, :]).astype(I32), axis=1)
        eid = jnp.arange(N_EXPERTS, dtype=I32)
        nonempty = counts > 0
        ordinal = jnp.cumsum(nonempty.astype(I32)) - 1
        later = nonempty[None, :] & (eid[None, :] > eid[:, None])
        nxt_e = jnp.min(jnp.where(later, eid[None, :], N_EXPERTS), axis=1)
        nxt_e = jnp.where(nxt_e == N_EXPERTS, -1, nxt_e)
        onehot = (be[:, None] == eid[None, :]).astype(I32)
        segment = jnp.sum(onehot * ordinal[None, :], axis=1)
        next_expert = jnp.sum(onehot * nxt_e[None, :], axis=1)
        dest = _dest(rec, pstart, tm=math.gcd(t, 2048))[:, 0:2].reshape(2 * t)

        xs = _dispatch(h2, dest, pstart, counts, n_used, cap, tm=512)
        ys = _experts(xs, expert_w1[l], expert_w3[l], expert_w2[l], be, n_used, segment, next_expert)
        assert l == depth - 1, "multi-layer stacking needs an un-normalised residual output"
        out = _final(x1, rec, ys, dest, final_norm_w, tm=512)
    return out.reshape(batch, seq, d)
```

```python
import functools
import math

import numpy as np
import jax
import jax.numpy as jnp
from jax import lax
from jax.experimental import pallas as pl
from jax.experimental.pallas import tpu as pltpu

F32 = jnp.float32
BF16 = jnp.bfloat16
I32 = jnp.int32

EPS = 1e-6
LANES = 128
SUBLANES = 8
HG_HEADS = 8
HG_DIM = 128
MLA_HEADS = 8
MLA_Q_RANK = 384
MLA_KV_RANK = 256
MLA_NOPE = 128
MLA_ROPE = 64
MLA_VDIM = 128
MLA_QK_DIM = MLA_NOPE + MLA_ROPE
MLA_QK_PAD = 256
ROPE_THETA = 10000.0
N_GROUPS = 8
EXPERTS_PER_GROUP = 8
N_EXPERTS = N_GROUPS * EXPERTS_PER_GROUP
EXPERT_LANE0 = N_GROUPS
EXPERT_FF = 512

HG_CHUNK = 128
HG_LOW_BLOCKS = (1, 2, 4)
HG_HIGH_BLOCKS = (8, 16, 32, 64)
HG_HEADS_PER_STEP = 2
HG_CHUNK_UNROLL = 8
COMBINE_SUB = 256
MOE_ROWS = 256
MOE_ROW_GROUPS = 2
DMA_UNROLL = 8
VMEM_LIMIT = 56 * 1024 * 1024


def _cparams(sem):
    return pltpu.CompilerParams(dimension_semantics=sem, vmem_limit_bytes=VMEM_LIMIT)


def _dot(a, b):
    return jnp.dot(a, b, preferred_element_type=F32)


def _dot_nt(a, b):
    return lax.dot_general(a, b, (((1,), (1,)), ((), ())), preferred_element_type=F32)


def _dot_tn(a, b):
    return lax.dot_general(a, b, (((0,), (0,)), ((), ())), preferred_element_type=F32)


def _split_bf16(x):
    hi = x.astype(BF16)
    lo = (x - hi.astype(F32)).astype(BF16)
    return hi, lo


_IN_OFF_Q = 0
_IN_OFF_F = 1024
_IN_OFF_I = 2048
_IN_OFF_G = 3072
_IN_OFF_CQ = 4096
_IN_OFF_CKV = _IN_OFF_CQ + MLA_Q_RANK
_IN_OFF_KR = _IN_OFF_CKV + MLA_KV_RANK
_IN_OFF_GA = _IN_OFF_KR + LANES
_IN_OFF_GB = _IN_OFF_GA + 1024
_IN_COLS_PAD = _IN_OFF_GB + 1024


def _rope_lane_layout(w):
    half = MLA_ROPE // 2
    z = jnp.zeros((w.shape[0], half), w.dtype)
    return jnp.concatenate([w[:, :half], z, w[:, half:], z], axis=1)


def _inproj_kernel(x_ref, nw_ref, lbt_ref, w_ref, q_ref, kk_ref, lf_ref, v_ref, g_ref, cq_ref, ckv_ref,
                   kr_ref, sa_ref, sb_ref, h_scr, *, layer):
    x = x_ref[...]
    ms = jnp.mean(x * x, axis=-1, keepdims=True)
    h_scr[...] = (x * lax.rsqrt(ms + EPS) * nw_ref[...]).astype(BF16)

    tab = lbt_ref[...]
    e = jnp.exp(tab - jnp.max(tab, axis=0, keepdims=True))
    sm = e / jnp.sum(e, axis=0, keepdims=True)
    lb = jnp.sum(sm[:layer + 1], axis=0, keepdims=True)

    def proj(off, width):
        return _dot(h_scr[...], w_ref[:, off:off + width])

    half = 512
    for c in range(0, 1024, half):
        z = proj(_IN_OFF_Q + c, half)
        q_ref[:, c:c + half] = (z * jax.nn.sigmoid(z) * (HG_DIM ** -0.5)).astype(BF16)
        z = proj(_IN_OFF_F + c, half)
        lbc = lb[:, c:c + half]
        fg = lbc + (1.0 - lbc) * jax.nn.sigmoid(z)
        lf_ref[:, c:c + half] = jnp.log2(fg)
        kk_ref[:, c:c + half] = (1.0 - fg).astype(BF16)
        v_ref[:, c:c + half] = proj(_IN_OFF_I + c, half).astype(BF16)
        z = proj(_IN_OFF_G + c, half)
        g_ref[:, c:c + half] = (z * jax.nn.sigmoid(z)).astype(BF16)
        sa_ref[:, c:c + half] = jax.nn.sigmoid(proj(_IN_OFF_GA + c, half)).astype(BF16)
        sb_ref[:, c:c + half] = jax.nn.sigmoid(proj(_IN_OFF_GB + c, half)).astype(BF16)
    cq_ref[...] = proj(_IN_OFF_CQ, MLA_Q_RANK)
    ckv_ref[...] = proj(_IN_OFF_CKV, MLA_KV_RANK)
    kr_ref[...] = proj(_IN_OFF_KR, LANES)


def _in_proj(x2, attn_norm_w, w_in, lb_table, layer, tm):
    t, d = x2.shape
    parts = []
    off = 0
    for width in (1024, 1024, 1024, 1024, MLA_Q_RANK, MLA_KV_RANK, MLA_ROPE, 1024, 1024):
        parts.append(w_in[:, off:off + width])
        off += width
    parts[6] = _rope_lane_layout(parts[6])
    w = jnp.concatenate([p.astype(BF16) for p in parts], axis=1)
    assert w.shape[1] == _IN_COLS_PAD
    row = lambda width: pl.BlockSpec((tm, width), lambda i: (i, 0))
    full = lambda a: pl.BlockSpec(a.shape, lambda i: (0,) * a.ndim)
    nw = attn_norm_w.reshape(1, d)
    out_shape = [jax.ShapeDtypeStruct((t, 1024), BF16),
                 jax.ShapeDtypeStruct((t, 1024), BF16),
                 jax.ShapeDtypeStruct((t, 1024), F32),
                 jax.ShapeDtypeStruct((t, 1024), BF16),
                 jax.ShapeDtypeStruct((t, 1024), BF16),
                 jax.ShapeDtypeStruct((t, MLA_Q_RANK), F32),
                 jax.ShapeDtypeStruct((t, MLA_KV_RANK), F32),
                 jax.ShapeDtypeStruct((t, LANES), F32),
                 jax.ShapeDtypeStruct((t, 1024), BF16),
                 jax.ShapeDtypeStruct((t, 1024), BF16)]
    return pl.pallas_call(
        functools.partial(_inproj_kernel, layer=layer),
        grid=(t // tm,),
        in_specs=[row(d), full(nw), full(lb_table),
                  pl.BlockSpec(w.shape, lambda i: (0, 0), pipeline_mode=pl.Buffered(1))],
        out_specs=[row(s.shape[1]) for s in out_shape],
        out_shape=out_shape,
        scratch_shapes=[pltpu.VMEM((tm, d), BF16)],
        compiler_params=_cparams(("parallel",)),
        name="in_proj",
    )(x2, nw, lb_table, w)


def _hgrn_tables():
    n = HG_CHUNK
    r = np.arange(n)[:, None]
    j = np.arange(n)[None, :]
    mats = [j <= r]
    for m in HG_LOW_BLOCKS:
        c = (r // (2 * m)) * (2 * m) + m - 1
        odd = (r & m) != 0
        mats.append(np.where(odd, (j > c) & (j <= r), (j > r) & (j <= c)))
    tab = np.concatenate(mats, axis=0).astype(np.float32)
    return jnp.asarray(np.concatenate([tab, tab], axis=1), dtype=BF16)


def _hgrn_kernel(q_ref, kk_ref, lf_ref, v_ref, g_ref, nw_ref, tab_ref, o_ref, st_ref, *, heads, n_chunks):
    n = HG_CHUNK
    unroll = math.gcd(n_chunks, HG_CHUNK_UNROLL)
    assert n == HG_DIM
    row = lax.broadcasted_iota(I32, (n, n), 0)
    col = lax.broadcasted_iota(I32, (n, n), 1)
    code = jnp.where(col <= row, 32 - lax.clz(row ^ col), -1)
    st_ref[...] = jnp.zeros_like(st_ref)
    nw = nw_ref[...]

    def load(r0, h):
        ls = slice(h * HG_DIM, (h + 1) * HG_DIM)
        qb = q_ref[pl.ds(r0, n), ls]
        kb = kk_ref[pl.ds(r0, n), ls]
        hi, lo = _split_bf16(lf_ref[pl.ds(r0, n), ls])
        ex = _dot(tab_ref[...], jnp.concatenate([hi, lo], axis=0))
        return dict(r0=r0, h=h, ls=ls, qb=qb, kb=kb, vb=v_ref[pl.ds(r0, n), ls], ex=ex)

    def low_products(p):
        q, k, ex = p["qb"].astype(F32), p["kb"].astype(F32), p["ex"]
        prods = [_dot_nt(p["qb"], p["kb"])]
        for i, m in enumerate(HG_LOW_BLOCKS):
            odd = (row & m) != 0
            w = jnp.where(odd, q, k) * jnp.exp2(ex[(i + 1) * n:(i + 2) * n, :])
            prods.append(_dot_nt(jnp.where(odd, w, 0.0).astype(BF16), jnp.where(odd, 0.0, w).astype(BF16)))
        p["low"] = prods

    def high_products(p):
        q, k, b = p["qb"].astype(F32), p["kb"].astype(F32), p["ex"][0:n, :]
        prods = []
        for m in HG_HIGH_BLOCKS:
            bases = range(0, n, 2 * m)
            e_parts, s_parts = [], []
            for base in bases:
                mid = b[base + m - 1:base + m, :]
                e_parts += [mid - b[base:base + m, :], b[base + m:base + 2 * m, :] - mid]
                s_parts += [k[base:base + m, :], q[base + m:base + 2 * m, :]]
            w = jnp.concatenate(s_parts, axis=0) * jnp.exp2(jnp.concatenate(e_parts, axis=0))
            zero = jnp.zeros((m, HG_DIM), F32)
            kp = jnp.concatenate([x for base in bases for x in (w[base:base + m, :], zero)], axis=0)
            qc = jnp.concatenate([w[base + m:base + 2 * m, :] for base in bases], axis=0)
            prods.append(_dot_nt(qc.astype(BF16), kp.astype(BF16)))
        p["high"] = prods

    def intra(p):
        a = jnp.where(code == 0, p["low"][0], 0.0)
        for m, r in zip(HG_LOW_BLOCKS, p["low"][1:]):
            a = jnp.where(code == m.bit_length(), r, a)
        for m, r in zip(HG_HIGH_BLOCKS, p["high"]):
            pieces = []
            for j, base in enumerate(range(0, n, 2 * m)):
                second = slice(base + m, base + 2 * m)
                pieces += [a[base:base + m, :],
                           jnp.where(code[second, :] == m.bit_length(), r[j * m:(j + 1) * m, :], a[second, :])]
            a = jnp.concatenate(pieces, axis=0)
        q, k, b = p["qb"].astype(F32), p["kb"].astype(F32), p["ex"][0:n, :]
        b_last = b[n - 1:n, :]
        p["o"] = _dot(a.astype(BF16), p["vb"])
        p["qe"] = (q * jnp.exp2(b)).astype(BF16)
        p["upd"] = _dot_tn(p["vb"], (k * jnp.exp2(b_last - b)).astype(BF16))
        p["decay"] = jnp.exp2(b_last)

    def finish(p, st):
        o = p["o"] + _dot_nt(p["qe"], st.astype(BF16))
        ms = jnp.mean(o * o, axis=-1, keepdims=True)
        y = o * lax.rsqrt(ms + EPS) * nw
        o_ref[pl.ds(p["r0"], n), p["ls"]] = (y * g_ref[pl.ds(p["r0"], n), p["ls"]].astype(F32)).astype(BF16)
        return st * p["decay"] + p["upd"]

    stages = (low_products, high_products, intra)

    def chunk_group(cg, carry):
        pairs = [(u, h) for u in range(unroll) for h in range(heads)]
        state = [st_ref[h] for h in range(heads)]
        live = {}
        for step in range(len(pairs) + len(stages) + 1):
            if step < len(pairs):
                u, h = pairs[step]
                live[step] = load(pl.multiple_of((cg * unroll + u) * n, n), h)
            for d, stage in enumerate(stages, start=1):
                if 0 <= step - d < len(pairs):
                    stage(live[step - d])
            done = step - len(stages) - 1
            if 0 <= done < len(pairs):
                h = pairs[done][1]
                state[h] = finish(live.pop(done), state[h])
        for h in range(heads):
            st_ref[h] = state[h]
        return carry

    lax.fori_loop(0, n_chunks // unroll, chunk_group, 0)


def _hgrn(q, kk, lf, v, g, out_norm_w, batch, seq):
    t, width = q.shape
    hp = HG_HEADS_PER_STEP
    wblk = hp * HG_DIM
    tab = _hgrn_tables()
    spec = pl.BlockSpec((seq, wblk), lambda b, j: (b, j))
    full = lambda a: pl.BlockSpec(a.shape, lambda b, j: (0,) * a.ndim)
    nw = out_norm_w.reshape(1, HG_DIM)
    return pl.pallas_call(
        functools.partial(_hgrn_kernel, heads=hp, n_chunks=seq // HG_CHUNK),
        grid=(batch, width // wblk),
        in_specs=[spec, spec, spec, spec, spec, full(nw), full(tab)],
        out_specs=spec,
        out_shape=jax.ShapeDtypeStruct((t, width), BF16),
        scratch_shapes=[pltpu.VMEM((hp, HG_DIM, HG_DIM), F32)],
        compiler_params=_cparams(("parallel", "parallel")),
        name="hgrn",
    )(q, kk, lf, v, g, nw, tab)


def _rope(p, cos_t, sin_t):
    return p * cos_t + pltpu.roll(p, LANES // 2, axis=1) * sin_t


def _rope_tables(pos, freq):
    rows = pos.shape[0]
    qr = rows // 4
    half = MLA_ROPE // 2
    lane = lax.broadcasted_iota(I32, (qr, LANES), 1)
    packed = jnp.where(lane < half, pos[0:qr], jnp.where(lane < 2 * half, pos[qr:2 * qr],
                       jnp.where(lane < 3 * half, pos[2 * qr:3 * qr], pos[3 * qr:])))
    ang = packed * freq
    cos_p, sin_p = jnp.cos(ang), jnp.sin(ang)
    first = lane < half
    second = (lane >= 2 * half) & (lane < 3 * half)
    roll = lambda x, shift: x if shift % LANES == 0 else pltpu.roll(x, shift % LANES, axis=1)
    cos_parts, sin_parts = [], []
    for j in range(4):
        to_first, to_second = -half * j, 2 * half - half * j
        cos_parts.append(jnp.where(first, roll(cos_p, to_first), jnp.where(second, roll(cos_p, to_second), 0.0)))
        sin_parts.append(jnp.where(first, -roll(sin_p, to_first), jnp.where(second, roll(sin_p, to_second), 0.0)))
    return jnp.concatenate(cos_parts, axis=0), jnp.concatenate(sin_parts, axis=0)


def _mla_prep_kernel(cq_ref, ckv_ref, kr_ref, pos_ref, qnw_ref, kvnw_ref, wq_ref, wkv_ref, rot_ref,
                     q_ref, k_ref, v_ref):
    cos_t, sin_t = _rope_tables(pos_ref[...], rot_ref[...])

    cq = cq_ref[...]
    cqn = cq * lax.rsqrt(jnp.mean(cq * cq, axis=-1, keepdims=True) + EPS) * qnw_ref[...]
    ckv = ckv_ref[...]
    ckvn = ckv * lax.rsqrt(jnp.mean(ckv * ckv, axis=-1, keepdims=True) + EPS) * kvnw_ref[...]
    cqn = cqn.astype(BF16)
    ckvn = ckvn.astype(BF16)
    k_pe = _rope(kr_ref[...], cos_t, sin_t).astype(BF16)
    scale = MLA_QK_DIM ** -0.5 * math.log2(math.e)
    for h in range(MLA_HEADS):
        c0 = h * MLA_QK_PAD
        qh = _dot(cqn, wq_ref[:, c0:c0 + MLA_QK_PAD])
        q_ref[h, :, 0:MLA_NOPE] = (qh[:, :MLA_NOPE] * scale).astype(BF16)
        q_ref[h, :, MLA_NOPE:MLA_QK_PAD] = (_rope(qh[:, MLA_NOPE:], cos_t, sin_t) * scale).astype(BF16)
        kvh = _dot(ckvn, wkv_ref[:, c0:c0 + MLA_NOPE + MLA_VDIM])
        k_ref[h, :, 0:MLA_NOPE] = kvh[:, :MLA_NOPE].astype(BF16)
        k_ref[h, :, MLA_NOPE:MLA_QK_PAD] = k_pe
        v_ref[h] = kvh[:, MLA_NOPE:].astype(BF16)


def _mla_prep(cq, ckv, kr, positions, q_norm_w, w_uq, kv_norm_w, w_ukv, batch, seq, tm):
    t = cq.shape[0]
    half = MLA_ROPE // 2
    wq = w_uq.reshape(MLA_Q_RANK, MLA_HEADS, MLA_QK_DIM)
    zq = jnp.zeros((MLA_Q_RANK, MLA_HEADS, half), w_uq.dtype)
    wq = jnp.concatenate([wq[:, :, :MLA_NOPE], wq[:, :, MLA_NOPE:MLA_NOPE + half], zq,
                          wq[:, :, MLA_NOPE + half:], zq], axis=2)
    wq = wq.reshape(MLA_Q_RANK, MLA_HEADS * MLA_QK_PAD).astype(BF16)
    wkv = w_ukv.astype(BF16)
    inv_freq = ROPE_THETA ** (-jnp.arange(half, dtype=F32) / half)
    rot = jnp.tile(inv_freq, LANES // half).reshape(1, LANES)
    pos = positions.astype(F32).reshape(t, 1)
    nblk = seq // tm
    row = lambda width: pl.BlockSpec((tm, width), lambda i: (i, 0))
    full = lambda a: pl.BlockSpec(a.shape, lambda i: (0,) * a.ndim)
    hspec = lambda width: pl.BlockSpec((None, MLA_HEADS, tm, width), lambda i: (i // nblk, 0, i % nblk, 0))
    qnw = q_norm_w.reshape(1, -1)
    kvnw = kv_norm_w.reshape(1, -1)
    return pl.pallas_call(
        _mla_prep_kernel,
        grid=(t // tm,),
        in_specs=[row(MLA_Q_RANK), row(MLA_KV_RANK), row(LANES), row(1), full(qnw), full(kvnw),
                  full(wq), full(wkv), full(rot)],
        out_specs=[hspec(MLA_QK_PAD), hspec(MLA_QK_PAD), hspec(MLA_VDIM)],
        out_shape=[jax.ShapeDtypeStruct((batch, MLA_HEADS, seq, MLA_QK_PAD), BF16),
                   jax.ShapeDtypeStruct((batch, MLA_HEADS, seq, MLA_QK_PAD), BF16),
                   jax.ShapeDtypeStruct((batch, MLA_HEADS, seq, MLA_VDIM), BF16)],
        compiler_params=_cparams(("parallel",)),
        name="mla_prep",
    )(cq, ckv, kr, pos, qnw, kvnw, wq, wkv, rot)


def _flash_kernel(q_ref, k_ref, v_ref, o_ref, v1_scr, *, tq):
    heads, seq = q_ref.shape[0], q_ref.shape[1]
    row = lax.broadcasted_iota(I32, (tq, tq), 0)
    col = lax.broadcasted_iota(I32, (tq, tq), 1)
    causal = row >= col
    items = [(h, i) for h in range(heads) for i in range(seq // tq)]

    def scores(h, i):
        q = q_ref[h, i * tq:(i + 1) * tq, :]
        return _dot_nt(q, k_ref[h, 0:(i + 1) * tq, :])

    def softmax(h, i, s):
        lo = i * tq
        sd = jnp.where(causal, s[:, lo:lo + tq], -jnp.inf)
        m = jnp.max(sd, axis=-1, keepdims=True)
        if i > 0:
            m = jnp.maximum(m, jnp.max(s[:, 0:lo], axis=-1, keepdims=True))
        p = jnp.exp2(jnp.concatenate([s[:, 0:lo], sd], axis=1) - m) if i > 0 else jnp.exp2(sd - m)
        return p.astype(BF16)

    v1_scr[:, :, 0:MLA_VDIM] = v_ref[...]
    v1_scr[:, :, MLA_VDIM:] = jnp.ones((heads, seq, MLA_VDIM), BF16)

    def output(h, i, p):
        hi = (i + 1) * tq
        acc = _dot(p, v1_scr[h, 0:hi, :])
        o_ref[i * tq:hi, h * MLA_VDIM:(h + 1) * MLA_VDIM] = (acc[:, 0:MLA_VDIM] / acc[:, MLA_VDIM:]).astype(BF16)

    s = {n: scores(*items[n]) for n in range(min(2, len(items)))}
    pending = softmax(*items[0], s.pop(0))
    for n in range(len(items)):
        if n + 2 < len(items):
            s[n + 2] = scores(*items[n + 2])
        output(*items[n], pending)
        if n + 1 < len(items):
            pending = softmax(*items[n + 1], s.pop(n + 1))


def _flash(q, k, v, tq, heads_per_step):
    batch, heads, seq, _ = q.shape
    hp = heads_per_step
    qk = pl.BlockSpec((None, hp, seq, MLA_QK_PAD), lambda b, h: (b, h, 0, 0))
    return pl.pallas_call(
        functools.partial(_flash_kernel, tq=tq),
        grid=(batch, heads // hp),
        in_specs=[qk, qk, pl.BlockSpec((None, hp, seq, MLA_VDIM), lambda b, h: (b, h, 0, 0))],
        out_specs=pl.BlockSpec((None, seq, hp * MLA_VDIM), lambda b, h: (b, 0, h)),
        out_shape=jax.ShapeDtypeStruct((batch, seq, heads * MLA_VDIM), BF16),
        scratch_shapes=[pltpu.VMEM((hp, seq, 2 * MLA_VDIM), BF16)],
        compiler_params=_cparams(("parallel", "parallel")),
        name="flash",
    )(q, k, v)


_R_E0, _R_E1, _R_W0, _R_W1, _R_RANK0, _R_RANK1 = 0, 1, 2, 3, 4, 5


def _first_lane_equal(x, value, lane):
    return jnp.min(jnp.where(x == value, lane, LANES), axis=-1, keepdims=True)


def _combine_kernel(oa_ref, ob_ref, sa_ref, sb_ref, x_ref, wa_ref, wb_ref, wo_ref, fnw_ref, wr_ref, br_ref,
                    lt_ref, x1_ref, h2_ref, rec_ref, cnt_ref, run_ref, y_scr):
    i = pl.program_id(0)

    @pl.when(i == 0)
    def _():
        run_ref[...] = jnp.zeros_like(run_ref)

    sub = lt_ref.shape[0]
    n_sub = x_ref.shape[0] // sub
    half = x_ref.shape[1] // 2
    rows = lambda s: slice(s * sub, (s + 1) * sub)

    def branches(s):
        for c in (0, half):
            cols = slice(c, c + half)
            ya = _dot(oa_ref[rows(s), :], wa_ref[:, cols])
            yb = _dot(ob_ref[rows(s), :], wb_ref[:, cols])
            y = sa_ref[rows(s), cols].astype(F32) * ya + sb_ref[rows(s), cols].astype(F32) * yb
            y_scr[rows(s), cols] = y.astype(BF16)

    def residual(s):
        for c in (0, half):
            cols = slice(c, c + half)
            x1_ref[rows(s), cols] = x_ref[rows(s), cols] + _dot(y_scr[rows(s), :], wo_ref[:, cols])

    def router(s):
        x1 = x1_ref[rows(s), :]
        h2 = x1 * lax.rsqrt(jnp.mean(x1 * x1, axis=-1, keepdims=True) + EPS) * fnw_ref[...]
        h2_ref[rows(s), :] = h2
        hh, hl = _split_bf16(h2)
        both = _dot(hh, wr_ref[...])
        return both[:, 0:LANES] + both[:, LANES:] + _dot(hl, wr_ref[:, 0:LANES]) + br_ref[...]

    run = run_ref[...]
    logits = {}
    for step in range(n_sub + 3):
        if step < n_sub:
            branches(step)
        if 0 <= step - 1 < n_sub:
            residual(step - 1)
        if 0 <= step - 2 < n_sub:
            logits[step - 2] = router(step - 2)
        if 0 <= step - 3 < n_sub:
            rec, run = _route(logits.pop(step - 3), run, lt_ref[...])
            rec_ref[rows(step - 3), :] = rec
    run_ref[...] = run
    cnt_ref[...] = run


def _route(logits, run, lt):
    tm = logits.shape[0]
    lane = lax.broadcasted_iota(I32, (tm, LANES), 1)
    neg = -jnp.inf

    gl = jnp.where(lane < N_GROUPS, logits, neg)
    ge = jnp.exp(gl - jnp.max(gl, axis=-1, keepdims=True))
    gp = ge / jnp.sum(ge, axis=-1, keepdims=True)
    g_w = jnp.max(gp, axis=-1, keepdims=True)
    g_idx = _first_lane_equal(gp, g_w, lane)

    ej = lane - EXPERT_LANE0
    sel = (ej >= 0) & (ej < N_EXPERTS) & ((ej >> 3) == g_idx)
    el = jnp.where(sel, logits, neg)
    ee = jnp.exp(el - jnp.max(el, axis=-1, keepdims=True))
    ep = jnp.where(sel, ee / jnp.sum(ee, axis=-1, keepdims=True), -1.0)
    p0 = jnp.max(ep, axis=-1, keepdims=True)
    l0 = _first_lane_equal(ep, p0, lane)
    ep1 = jnp.where(lane == l0, -1.0, ep)
    p1 = jnp.max(ep1, axis=-1, keepdims=True)
    l1 = _first_lane_equal(ep1, p1, lane)
    psum = p0 + p1
    w0 = g_w * (p0 / psum)
    w1 = g_w * (p1 / psum)

    oh0 = lane == l0
    oh1 = lane == l1
    oh = jnp.where(oh0 | oh1, 1.0, 0.0)
    before = run + _dot(lt, oh.astype(BF16))
    rank0 = jnp.sum(jnp.where(oh0, before, 0.0), axis=-1, keepdims=True)
    rank1 = jnp.sum(jnp.where(oh1, before, 0.0), axis=-1, keepdims=True)
    run = run + jnp.sum(oh, axis=0, keepdims=True)

    rec = jnp.where(lane == _R_E0, (l0 - EXPERT_LANE0).astype(F32), 0.0)
    rec = jnp.where(lane == _R_E1, (l1 - EXPERT_LANE0).astype(F32), rec)
    rec = jnp.where(lane == _R_W0, w0, rec)
    rec = jnp.where(lane == _R_W1, w1, rec)
    rec = jnp.where(lane == _R_RANK0, rank0, rec)
    rec = jnp.where(lane == _R_RANK1, rank1, rec)
    return rec, run


def _combine(o_a, o_b, sa, sb, x2, w_a, w_b, w_o, ffn_norm_w, w_group, b_group, w_expert, b_expert, tm):
    t, d = x2.shape
    pad = LANES - N_GROUPS - N_EXPERTS
    wr = jnp.concatenate([w_group, w_expert, jnp.zeros((d, pad), F32)], axis=1)
    wr_hi = wr.astype(BF16)
    wr_lo = (wr - wr_hi.astype(F32)).astype(BF16)
    wr2 = jnp.concatenate([wr_hi, wr_lo], axis=1)
    br = jnp.concatenate([b_group, b_expert, jnp.zeros((pad,), F32)]).reshape(1, LANES)
    lt = jnp.asarray(np.tril(np.ones((COMBINE_SUB, COMBINE_SUB), np.float32), -1), dtype=BF16)
    fnw = ffn_norm_w.reshape(1, d)
    row = lambda width: pl.BlockSpec((tm, width), lambda i: (i, 0))
    full = lambda a: pl.BlockSpec(a.shape, lambda i: (0,) * a.ndim)
    wa, wb, wo = w_a.astype(BF16), w_b.astype(BF16), w_o.astype(BF16)
    return pl.pallas_call(
        _combine_kernel,
        grid=(t // tm,),
        in_specs=[row(d), row(d), row(d), row(d), row(d), full(wa), full(wb), full(wo), full(fnw),
                  full(wr2), full(br), full(lt)],
        out_specs=[row(d), row(d), row(LANES), pl.BlockSpec((1, LANES), lambda i: (0, 0))],
        out_shape=[jax.ShapeDtypeStruct((t, d), F32),
                   jax.ShapeDtypeStruct((t, d), F32),
                   jax.ShapeDtypeStruct((t, LANES), F32),
                   jax.ShapeDtypeStruct((1, LANES), F32)],
        scratch_shapes=[pltpu.VMEM((1, LANES), F32), pltpu.VMEM((tm, d), BF16)],
        compiler_params=_cparams(("arbitrary",)),
        name="combine",
    )(o_a, o_b, sa, sb, x2, wa, wb, wo, fnw, wr2, br, lt)


def _row_copy(src_ref, src_row, dst_ref, dst_row, sem):
    return pltpu.make_async_copy(src_ref.at[pl.ds(src_row, 1)], dst_ref.at[pl.ds(dst_row, 1)], sem)


def _dest_kernel(rec_ref, ps_ref, d_ref):
    rec = rec_ref[...]
    lane = lax.broadcasted_iota(I32, rec.shape, 1)
    ps = ps_ref[...]

    def lookup(e_lane, r_lane):
        e = rec[:, e_lane:e_lane + 1].astype(I32) + EXPERT_LANE0
        start = jnp.sum(jnp.where(lane == e, ps, 0.0), axis=-1, keepdims=True)
        return (start + rec[:, r_lane:r_lane + 1]).astype(I32)

    d0 = lookup(_R_E0, _R_RANK0)
    d1 = lookup(_R_E1, _R_RANK1)
    d_ref[...] = jnp.where(lane == 0, d0, jnp.where(lane == 1, d1, 0))


def _dest(rec, pstart, tm):
    t = rec.shape[0]
    ps = jnp.zeros((1, LANES), F32).at[0, EXPERT_LANE0:EXPERT_LANE0 + N_EXPERTS].set(pstart.astype(F32))
    return pl.pallas_call(
        _dest_kernel,
        grid=(t // tm,),
        in_specs=[pl.BlockSpec((tm, LANES), lambda i: (i, 0)), pl.BlockSpec((1, LANES), lambda i: (0, 0))],
        out_specs=pl.BlockSpec((tm, LANES), lambda i: (i, 0)),
        out_shape=jax.ShapeDtypeStruct((t, LANES), I32),
        compiler_params=_cparams(("parallel",)),
        name="dest",
    )(rec, ps)


def _dispatch_kernel(ps_ref, cnt_ref, nu_ref, d_ref, h_ref, xs_ref, zero_scr, sem, zsem):
    tm = h_ref.shape[0]
    bm = zero_scr.shape[0]

    def zero_fill(wait):
        def expert(e, carry):
            first = ps_ref[e] + cnt_ref[e]
            end = first + ((bm - (cnt_ref[e] & (bm - 1))) & (bm - 1))
            tiles = jnp.minimum((first + SUBLANES - 1) & -SUBLANES, end)

            def row(r, c):
                copy = _row_copy(zero_scr, 0, xs_ref, r, zsem)
                copy.wait() if wait else copy.start()
                return c

            def tile(j, c):
                r = pl.multiple_of(tiles + j * SUBLANES, SUBLANES)
                copy = pltpu.make_async_copy(zero_scr.at[pl.ds(0, SUBLANES)], xs_ref.at[pl.ds(r, SUBLANES)], zsem)
                copy.wait() if wait else copy.start()
                return c

            lax.fori_loop(first, tiles, row, 0)
            lax.fori_loop(0, (end - tiles) >> (SUBLANES.bit_length() - 1), tile, 0)
            return carry

        lax.fori_loop(0, cnt_ref.shape[0], expert, 0)

        def tail(blk, carry):
            copy = pltpu.make_async_copy(zero_scr, xs_ref.at[pl.ds(blk * bm, bm)], zsem)
            copy.wait() if wait else copy.start()
            return carry

        lax.fori_loop(nu_ref[0], xs_ref.shape[0] // bm, tail, 0)

    @pl.when(pl.program_id(0) == 0)
    def _():
        zero_scr[...] = jnp.zeros_like(zero_scr)
        zero_fill(wait=False)

    def start(j, carry):
        for u in range(DMA_UNROLL):
            tok = j * DMA_UNROLL + u
            for k in range(2):
                _row_copy(h_ref, tok, xs_ref, d_ref[2 * tok + k], sem).start(priority=k)
        return carry

    lax.fori_loop(0, tm // DMA_UNROLL, start, 0)

    def wait(j, carry):
        for _ in range(2 * DMA_UNROLL):
            _row_copy(h_ref, 0, xs_ref, 0, sem).wait()
        return carry

    lax.fori_loop(0, tm // DMA_UNROLL, wait, 0)

    @pl.when(pl.program_id(0) == pl.num_programs(0) - 1)
    def _():
        zero_fill(wait=True)


def _dispatch(h2, dest, pstart, counts, n_used, cap, tm):
    t, d = h2.shape
    return pl.pallas_call(
        _dispatch_kernel,
        grid_spec=pltpu.PrefetchScalarGridSpec(
            num_scalar_prefetch=3, grid=(t // tm,),
            in_specs=[pl.BlockSpec((2 * tm,), lambda i, ps, cnt, nu: (i,), memory_space=pltpu.SMEM),
                      pl.BlockSpec((tm, d), lambda i, ps, cnt, nu: (i, 0))],
            out_specs=pl.BlockSpec(memory_space=pl.ANY),
            scratch_shapes=[pltpu.VMEM((MOE_ROWS, d), h2.dtype), pltpu.SemaphoreType.DMA(()),
                            pltpu.SemaphoreType.DMA(())]),
        out_shape=jax.ShapeDtypeStruct((cap, d), h2.dtype),
        compiler_params=_cparams(("arbitrary",)),
        name="dispatch",
    )(pstart, counts, n_used, dest, h2)


def _experts_kernel(be_ref, nu_ref, seg_ref, nxt_ref, xs_ref, w1_hbm, w3_hbm, w2_hbm, ys_ref,
                    w1f, w3f, w2f, w1b, w3b, w2b, sems):
    i = pl.program_id(0)

    def weight_copies(e, slot):
        return [pltpu.make_async_copy(hbm.at[e], buf.at[slot], sems.at[slot, j])
                for j, (hbm, buf) in enumerate(((w1_hbm, w1f), (w3_hbm, w3f), (w2_hbm, w2f)))]

    @pl.when(i < nu_ref[0])
    def _():
        changed = jnp.logical_or(i == 0, be_ref[i] != be_ref[jnp.maximum(i - 1, 0)])
        slot = seg_ref[i] & 1

        @pl.when(i == 0)
        def _():
            for copy in weight_copies(be_ref[0], 0):
                copy.start()

        @pl.when(changed)
        def _():
            for copy in weight_copies(be_ref[i], slot):
                copy.wait()
            w1b[...] = w1f[slot].astype(BF16)
            w3b[...] = w3f[slot].astype(BF16)
            w2b[...] = w2f[slot].astype(BF16)

            @pl.when(nxt_ref[i] >= 0)
            def _():
                for copy in weight_copies(nxt_ref[i], 1 - slot):
                    copy.start()

        grp = xs_ref.shape[0] // MOE_ROW_GROUPS
        rows = lambda g: slice(g * grp, (g + 1) * grp)

        def up(g):
            x = xs_ref[rows(g), :].astype(BF16)
            return _dot(x, w1b[...]), _dot(x, w3b[...])

        def down(g, h1, h3):
            hid = h1 * jax.nn.sigmoid(h1) * h3
            ys_ref[rows(g), :] = _dot(hid.astype(BF16), w2b[...])

        nxt = up(0)
        for g in range(MOE_ROW_GROUPS):
            cur = nxt
            if g + 1 < MOE_ROW_GROUPS:
                nxt = up(g + 1)
            down(g, *cur)

    @pl.when(i >= nu_ref[0])
    def _():
        ys_ref[...] = jnp.zeros_like(ys_ref)


def _experts(xs, w1, w3, w2, block_expert, n_used, segment, next_expert):
    cap, d = xs.shape
    bm = MOE_ROWS
    ff = w1.shape[2]
    hbm = pl.BlockSpec(memory_space=pl.ANY)
    return pl.pallas_call(
        _experts_kernel,
        grid_spec=pltpu.PrefetchScalarGridSpec(
            num_scalar_prefetch=4, grid=(cap // bm,),
            in_specs=[pl.BlockSpec((bm, d), lambda i, be, nu, seg, nxt: (jnp.minimum(i, nu[0] - 1), 0)),
                      hbm, hbm, hbm],
            out_specs=pl.BlockSpec((bm, d), lambda i, be, nu, seg, nxt: (i, 0)),
            scratch_shapes=[pltpu.VMEM((2, d, ff), F32), pltpu.VMEM((2, d, ff), F32), pltpu.VMEM((2, ff, d), F32),
                            pltpu.VMEM((d, ff), BF16), pltpu.VMEM((d, ff), BF16), pltpu.VMEM((ff, d), BF16),
                            pltpu.SemaphoreType.DMA((2, 3))]),
        out_shape=jax.ShapeDtypeStruct((cap, d), F32),
        compiler_params=_cparams(("arbitrary",)),
        name="experts",
    )(block_expert, n_used, segment, next_expert, xs, w1, w3, w2)


def _final_kernel(d_ref, dn_ref, x1_ref, rec_ref, nw_ref, ys_ref, o_ref, buf, sems):
    i = pl.program_id(0)
    tm = x1_ref.shape[0]
    slot = i & 1

    def gather(idx_ref, s):
        def start(j, carry):
            for u in range(DMA_UNROLL):
                tok = j * DMA_UNROLL + u
                for k in range(2):
                    _row_copy(ys_ref, idx_ref[2 * tok + k], buf.at[s, k], tok, sems.at[s]).start(priority=k)
            return carry

        lax.fori_loop(0, tm // DMA_UNROLL, start, 0)

    @pl.when(i == 0)
    def _():
        gather(d_ref, 0)

    @pl.when(i + 1 < pl.num_programs(0))
    def _():
        gather(dn_ref, 1 - slot)

    def wait(j, carry):
        for _ in range(2 * DMA_UNROLL):
            _row_copy(ys_ref, 0, buf.at[slot, 0], 0, sems.at[slot]).wait()
        return carry

    lax.fori_loop(0, tm // DMA_UNROLL, wait, 0)

    rec = rec_ref[...]
    x = x1_ref[...] + buf[slot, 0] * rec[:, _R_W0:_R_W0 + 1] + buf[slot, 1] * rec[:, _R_W1:_R_W1 + 1]
    o_ref[...] = x * lax.rsqrt(jnp.mean(x * x, axis=-1, keepdims=True) + EPS) * nw_ref[...]


def _final(x1, rec, ys, dest, final_norm_w, tm):
    t, d = x1.shape
    nw = final_norm_w.reshape(1, d)
    last = t // tm - 1
    return pl.pallas_call(
        _final_kernel,
        grid=(t // tm,),
        in_specs=[pl.BlockSpec((2 * tm,), lambda i: (i,), memory_space=pltpu.SMEM),
                  pl.BlockSpec((2 * tm,), lambda i: (jnp.minimum(i + 1, last),), memory_space=pltpu.SMEM),
                  pl.BlockSpec((tm, d), lambda i: (i, 0)),
                  pl.BlockSpec((tm, LANES), lambda i: (i, 0)),
                  pl.BlockSpec((1, d), lambda i: (0, 0)),
                  pl.BlockSpec(memory_space=pl.ANY)],
        out_specs=pl.BlockSpec((tm, d), lambda i: (i, 0)),
        out_shape=jax.ShapeDtypeStruct((t, d), F32),
        scratch_shapes=[pltpu.VMEM((2, 2, tm, d), F32), pltpu.SemaphoreType.DMA((2,))],
        compiler_params=_cparams(("arbitrary",)),
        name="final",
    )(dest, dest, x1, rec, nw, ys)


def _moe_layout(counts):
    bm = MOE_ROWS
    padded = (counts + bm - 1) // bm * bm
    pends = jnp.cumsum(padded)
    pstart = pends - padded
    return pstart.astype(I32), pends.astype(I32)


def kernel(x, positions, attn_norm_w, w_in, hg_lower_bound, hg_out_norm_w, mla_q_norm_w, mla_w_uq, mla_kv_norm_w, mla_w_ukv, w_branch_hgrn, w_branch_mla, w_out, ffn_norm_w, router_group_w, router_group_b, router_expert_w, router_expert_b, expert_w1, expert_w3, expert_w2, final_norm_w):
    batch, seq, d = x.shape
    t = batch * seq
    depth = w_in.shape[0]
    assert d == 1024 and seq % 512 == 0 and t % 1024 == 0
    x2 = x.reshape(t, d)
    for l in range(depth):
        q, kk, lf, v, g, cq, ckv, kr, sa, sb = _in_proj(x2, attn_norm_w[l], w_in[l], hg_lower_bound, l, tm=256)
        o_a = _hgrn(q, kk, lf, v, g, hg_out_norm_w[l], batch, seq)
        mq, mk, mv = _mla_prep(cq, ckv, kr, positions, mla_q_norm_w[l], mla_w_uq[l], mla_kv_norm_w[l],
                               mla_w_ukv[l], batch, seq, tm=512)
        o_b = _flash(mq, mk, mv, tq=256, heads_per_step=4).reshape(t, MLA_HEADS * MLA_VDIM)
        x1, h2, rec, cnt = _combine(o_a, o_b, sa, sb, x2, w_branch_hgrn[l], w_branch_mla[l], w_out[l],
                                    ffn_norm_w[l], router_group_w[l], router_group_b[l],
                                    router_expert_w[l], router_expert_b[l], tm=1024)

        counts = cnt[0, EXPERT_LANE0:EXPERT_LANE0 + N_EXPERTS].astype(I32)
        pstart, pends = _moe_layout(counts)
        n_blocks = (2 * t) // MOE_ROWS + N_EXPERTS
        cap = n_blocks * MOE_ROWS
        n_used = (pends[-1] // MOE_ROWS).reshape(1)
        blk = jnp.minimum(jnp.arange(n_blocks, dtype=I32), n_used[0] - 1)
        be = jnp.sum((blk[:, None] * MOE_ROWS >= pends[None, :]).astype(I32), axis=1)
        eid = jnp.arange(N_EXPERTS, dtype=I32)
        nonempty = counts > 0
        ordinal = jnp.cumsum(nonempty.astype(I32)) - 1
        later = nonempty[None, :] & (eid[None, :] > eid[:, None])
        nxt_e = jnp.min(jnp.where(later, eid[None, :], N_EXPERTS), axis=1)
        nxt_e = jnp.where(nxt_e == N_EXPERTS, -1, nxt_e)
        onehot = (be[:, None] == eid[None, :]).astype(I32)
        segment = jnp.sum(onehot * ordinal[None, :], axis=1)
        next_expert = jnp.sum(onehot * nxt_e[None, :], axis=1)
        dest = _dest(rec, pstart, tm=math.gcd(t, 2048))[:, 0:2].reshape(2 * t)

        xs = _dispatch(h2, dest, pstart, counts, n_used, cap, tm=512)
        ys = _experts(xs, expert_w1[l], expert_w3[l], expert_w2[l], be, n_used, segment, next_expert)
        assert l == depth - 1, "multi-layer stacking needs an un-normalised residual output"
        out = _final(x1, rec, ys, dest, final_norm_w, tm=512)
    return out.reshape(batch, seq, d)
```

```python
import functools
import math

import numpy as np
import jax
import jax.numpy as jnp
from jax import lax
from jax.experimental import pallas as pl
from jax.experimental.pallas import tpu as pltpu

F32 = jnp.float32
BF16 = jnp.bfloat16
I32 = jnp.int32

EPS = 1e-6
LANES = 128
SUBLANES = 8
HG_HEADS = 8
HG_DIM = 128
MLA_HEADS = 8
MLA_Q_RANK = 384
MLA_KV_RANK = 256
MLA_NOPE = 128
MLA_ROPE = 64
MLA_VDIM = 128
MLA_QK_DIM = MLA_NOPE + MLA_ROPE
MLA_QK_PAD = 256
ROPE_THETA = 10000.0
N_GROUPS = 8
EXPERTS_PER_GROUP = 8
N_EXPERTS = N_GROUPS * EXPERTS_PER_GROUP
EXPERT_LANE0 = N_GROUPS
EXPERT_FF = 512

HG_CHUNK = 128
HG_LOW_BLOCKS = (1, 2, 4)
HG_HIGH_BLOCKS = (8, 16, 32, 64)
HG_HEADS_PER_STEP = 4
HG_CHUNK_UNROLL = 8
COMBINE_SUB = 256
MOE_ROWS = 256
MOE_ROW_GROUPS = 2
DMA_UNROLL = 8
VMEM_LIMIT = 56 * 1024 * 1024


def _cparams(sem):
    return pltpu.CompilerParams(dimension_semantics=sem, vmem_limit_bytes=VMEM_LIMIT)


def _dot(a, b):
    return jnp.dot(a, b, preferred_element_type=F32)


def _dot_nt(a, b):
    return lax.dot_general(a, b, (((1,), (1,)), ((), ())), preferred_element_type=F32)


def _dot_tn(a, b):
    return lax.dot_general(a, b, (((0,), (0,)), ((), ())), preferred_element_type=F32)


def _split_bf16(x):
    hi = x.astype(BF16)
    lo = (x - hi.astype(F32)).astype(BF16)
    return hi, lo


_IN_OFF_Q = 0
_IN_OFF_F = 1024
_IN_OFF_I = 2048
_IN_OFF_G = 3072
_IN_OFF_CQ = 4096
_IN_OFF_CKV = _IN_OFF_CQ + MLA_Q_RANK
_IN_OFF_KR = _IN_OFF_CKV + MLA_KV_RANK
_IN_OFF_GA = _IN_OFF_KR + LANES
_IN_OFF_GB = _IN_OFF_GA + 1024
_IN_COLS_PAD = _IN_OFF_GB + 1024


def _rope_lane_layout(w):
    half = MLA_ROPE // 2
    z = jnp.zeros((w.shape[0], half), w.dtype)
    return jnp.concatenate([w[:, :half], z, w[:, half:], z], axis=1)


def _inproj_kernel(x_ref, nw_ref, lbt_ref, w_ref, q_ref, kk_ref, lf_ref, v_ref, g_ref, cq_ref, ckv_ref,
                   kr_ref, sa_ref, sb_ref, h_scr, *, layer):
    x = x_ref[...]
    ms = jnp.mean(x * x, axis=-1, keepdims=True)
    h_scr[...] = (x * lax.rsqrt(ms + EPS) * nw_ref[...]).astype(BF16)

    tab = lbt_ref[...]
    e = jnp.exp(tab - jnp.max(tab, axis=0, keepdims=True))
    sm = e / jnp.sum(e, axis=0, keepdims=True)
    lb = jnp.sum(sm[:layer + 1], axis=0, keepdims=True)

    def proj(off, width):
        return _dot(h_scr[...], w_ref[:, off:off + width])

    half = 512
    for c in range(0, 1024, half):
        z = proj(_IN_OFF_Q + c, half)
        q_ref[:, c:c + half] = (z * jax.nn.sigmoid(z) * (HG_DIM ** -0.5)).astype(BF16)
        z = proj(_IN_OFF_F + c, half)
        lbc = lb[:, c:c + half]
        fg = lbc + (1.0 - lbc) * jax.nn.sigmoid(z)
        lf_ref[:, c:c + half] = jnp.log2(fg)
        kk_ref[:, c:c + half] = (1.0 - fg).astype(BF16)
        v_ref[:, c:c + half] = proj(_IN_OFF_I + c, half).astype(BF16)
        z = proj(_IN_OFF_G + c, half)
        g_ref[:, c:c + half] = (z * jax.nn.sigmoid(z)).astype(BF16)
        sa_ref[:, c:c + half] = jax.nn.sigmoid(proj(_IN_OFF_GA + c, half)).astype(BF16)
        sb_ref[:, c:c + half] = jax.nn.sigmoid(proj(_IN_OFF_GB + c, half)).astype(BF16)
    cq_ref[...] = proj(_IN_OFF_CQ, MLA_Q_RANK)
    ckv_ref[...] = proj(_IN_OFF_CKV, MLA_KV_RANK)
    kr_ref[...] = proj(_IN_OFF_KR, LANES)


def _in_proj(x2, attn_norm_w, w_in, lb_table, layer, tm):
    t, d = x2.shape
    parts = []
    off = 0
    for width in (1024, 1024, 1024, 1024, MLA_Q_RANK, MLA_KV_RANK, MLA_ROPE, 1024, 1024):
        parts.append(w_in[:, off:off + width])
        off += width
    parts[6] = _rope_lane_layout(parts[6])
    w = jnp.concatenate([p.astype(BF16) for p in parts], axis=1)
    assert w.shape[1] == _IN_COLS_PAD
    row = lambda width: pl.BlockSpec((tm, width), lambda i: (i, 0))
    full = lambda a: pl.BlockSpec(a.shape, lambda i: (0,) * a.ndim)
    nw = attn_norm_w.reshape(1, d)
    out_shape = [jax.ShapeDtypeStruct((t, 1024), BF16),
                 jax.ShapeDtypeStruct((t, 1024), BF16),
                 jax.ShapeDtypeStruct((t, 1024), F32),
                 jax.ShapeDtypeStruct((t, 1024), BF16),
                 jax.ShapeDtypeStruct((t, 1024), BF16),
                 jax.ShapeDtypeStruct((t, MLA_Q_RANK), F32),
                 jax.ShapeDtypeStruct((t, MLA_KV_RANK), F32),
                 jax.ShapeDtypeStruct((t, LANES), F32),
                 jax.ShapeDtypeStruct((t, 1024), BF16),
                 jax.ShapeDtypeStruct((t, 1024), BF16)]
    return pl.pallas_call(
        functools.partial(_inproj_kernel, layer=layer),
        grid=(t // tm,),
        in_specs=[row(d), full(nw), full(lb_table),
                  pl.BlockSpec(w.shape, lambda i: (0, 0), pipeline_mode=pl.Buffered(1))],
        out_specs=[row(s.shape[1]) for s in out_shape],
        out_shape=out_shape,
        scratch_shapes=[pltpu.VMEM((tm, d), BF16)],
        compiler_params=_cparams(("parallel",)),
        name="in_proj",
    )(x2, nw, lb_table, w)


def _hgrn_tables():
    n = HG_CHUNK
    r = np.arange(n)[:, None]
    j = np.arange(n)[None, :]
    mats = [j <= r]
    for m in HG_LOW_BLOCKS:
        c = (r // (2 * m)) * (2 * m) + m - 1
        odd = (r & m) != 0
        mats.append(np.where(odd, (j > c) & (j <= r), (j > r) & (j <= c)))
    tab = np.concatenate(mats, axis=0).astype(np.float32)
    return jnp.asarray(np.concatenate([tab, tab], axis=1), dtype=BF16)


def _hgrn_kernel(q_ref, kk_ref, lf_ref, v_ref, g_ref, nw_ref, tab_ref, o_ref, st_ref, *, heads, n_chunks):
    n = HG_CHUNK
    unroll = math.gcd(n_chunks, HG_CHUNK_UNROLL)
    assert n == HG_DIM
    row = lax.broadcasted_iota(I32, (n, n), 0)
    col = lax.broadcasted_iota(I32, (n, n), 1)
    code = jnp.where(col <= row, 32 - lax.clz(row ^ col), -1)
    st_ref[...] = jnp.zeros_like(st_ref)
    nw = nw_ref[...]

    def load(r0, h):
        ls = slice(h * HG_DIM, (h + 1) * HG_DIM)
        qb = q_ref[pl.ds(r0, n), ls]
        kb = kk_ref[pl.ds(r0, n), ls]
        hi, lo = _split_bf16(lf_ref[pl.ds(r0, n), ls])
        ex = _dot(tab_ref[...], jnp.concatenate([hi, lo], axis=0))
        return dict(r0=r0, h=h, ls=ls, qb=qb, kb=kb, vb=v_ref[pl.ds(r0, n), ls], ex=ex)

    def low_products(p):
        q, k, ex = p["qb"].astype(F32), p["kb"].astype(F32), p["ex"]
        prods = [_dot_nt(p["qb"], p["kb"])]
        for i, m in enumerate(HG_LOW_BLOCKS):
            odd = (row & m) != 0
            w = jnp.where(odd, q, k) * jnp.exp2(ex[(i + 1) * n:(i + 2) * n, :])
            prods.append(_dot_nt(jnp.where(odd, w, 0.0).astype(BF16), jnp.where(odd, 0.0, w).astype(BF16)))
        p["low"] = prods

    def high_products(p):
        q, k, b = p["qb"].astype(F32), p["kb"].astype(F32), p["ex"][0:n, :]
        prods = []
        for m in HG_HIGH_BLOCKS:
            bases = range(0, n, 2 * m)
            e_parts, s_parts = [], []
            for base in bases:
                mid = b[base + m - 1:base + m, :]
                e_parts += [mid - b[base:base + m, :], b[base + m:base + 2 * m, :] - mid]
                s_parts += [k[base:base + m, :], q[base + m:base + 2 * m, :]]
            w = jnp.concatenate(s_parts, axis=0) * jnp.exp2(jnp.concatenate(e_parts, axis=0))
            zero = jnp.zeros((m, HG_DIM), F32)
            kp = jnp.concatenate([x for base in bases for x in (w[base:base + m, :], zero)], axis=0)
            qc = jnp.concatenate([w[base + m:base + 2 * m, :] for base in bases], axis=0)
            prods.append(_dot_nt(qc.astype(BF16), kp.astype(BF16)))
        p["high"] = prods

    def intra(p):
        a = jnp.where(code == 0, p["low"][0], 0.0)
        for m, r in zip(HG_LOW_BLOCKS, p["low"][1:]):
            a = jnp.where(code == m.bit_length(), r, a)
        for m, r in zip(HG_HIGH_BLOCKS, p["high"]):
            pieces = []
            for j, base in enumerate(range(0, n, 2 * m)):
                second = slice(base + m, base + 2 * m)
                pieces += [a[base:base + m, :],
                           jnp.where(code[second, :] == m.bit_length(), r[j * m:(j + 1) * m, :], a[second, :])]
            a = jnp.concatenate(pieces, axis=0)
        q, k, b = p["qb"].astype(F32), p["kb"].astype(F32), p["ex"][0:n, :]
        b_last = b[n - 1:n, :]
        p["o"] = _dot(a.astype(BF16), p["vb"])
        p["qe"] = (q * jnp.exp2(b)).astype(BF16)
        p["upd"] = _dot_tn(p["vb"], (k * jnp.exp2(b_last - b)).astype(BF16))
        p["decay"] = jnp.exp2(b_last)

    def finish(p, st):
        o = p["o"] + _dot_nt(p["qe"], st.astype(BF16))
        ms = jnp.mean(o * o, axis=-1, keepdims=True)
        y = o * lax.rsqrt(ms + EPS) * nw
        o_ref[pl.ds(p["r0"], n), p["ls"]] = (y * g_ref[pl.ds(p["r0"], n), p["ls"]].astype(F32)).astype(BF16)
        return st * p["decay"] + p["upd"]

    stages = (low_products, high_products, intra)

    def chunk_group(cg, carry):
        pairs = [(u, h) for u in range(unroll) for h in range(heads)]
        state = [st_ref[h] for h in range(heads)]
        live = {}
        for step in range(len(pairs) + len(stages) + 1):
            if step < len(pairs):
                u, h = pairs[step]
                live[step] = load(pl.multiple_of((cg * unroll + u) * n, n), h)
            for d, stage in enumerate(stages, start=1):
                if 0 <= step - d < len(pairs):
                    stage(live[step - d])
            done = step - len(stages) - 1
            if 0 <= done < len(pairs):
                h = pairs[done][1]
                state[h] = finish(live.pop(done), state[h])
        for h in range(heads):
            st_ref[h] = state[h]
        return carry

    lax.fori_loop(0, n_chunks // unroll, chunk_group, 0)


def _hgrn(q, kk, lf, v, g, out_norm_w, batch, seq):
    t, width = q.shape
    hp = HG_HEADS_PER_STEP
    wblk = hp * HG_DIM
    tab = _hgrn_tables()
    spec = pl.BlockSpec((seq, wblk), lambda b, j: (b, j))
    full = lambda a: pl.BlockSpec(a.shape, lambda b, j: (0,) * a.ndim)
    nw = out_norm_w.reshape(1, HG_DIM)
    return pl.pallas_call(
        functools.partial(_hgrn_kernel, heads=hp, n_chunks=seq // HG_CHUNK),
        grid=(batch, width // wblk),
        in_specs=[spec, spec, spec, spec, spec, full(nw), full(tab)],
        out_specs=spec,
        out_shape=jax.ShapeDtypeStruct((t, width), BF16),
        scratch_shapes=[pltpu.VMEM((hp, HG_DIM, HG_DIM), F32)],
        compiler_params=_cparams(("parallel", "parallel")),
        name="hgrn",
    )(q, kk, lf, v, g, nw, tab)


def _rope(p, cos_t, sin_t):
    return p * cos_t + pltpu.roll(p, LANES // 2, axis=1) * sin_t


def _rope_tables(pos, freq):
    rows = pos.shape[0]
    qr = rows // 4
    half = MLA_ROPE // 2
    lane = lax.broadcasted_iota(I32, (qr, LANES), 1)
    packed = jnp.where(lane < half, pos[0:qr], jnp.where(lane < 2 * half, pos[qr:2 * qr],
                       jnp.where(lane < 3 * half, pos[2 * qr:3 * qr], pos[3 * qr:])))
    ang = packed * freq
    cos_p, sin_p = jnp.cos(ang), jnp.sin(ang)
    first = lane < half
    second = (lane >= 2 * half) & (lane < 3 * half)
    roll = lambda x, shift: x if shift % LANES == 0 else pltpu.roll(x, shift % LANES, axis=1)
    cos_parts, sin_parts = [], []
    for j in range(4):
        to_first, to_second = -half * j, 2 * half - half * j
        cos_parts.append(jnp.where(first, roll(cos_p, to_first), jnp.where(second, roll(cos_p, to_second), 0.0)))
        sin_parts.append(jnp.where(first, -roll(sin_p, to_first), jnp.where(second, roll(sin_p, to_second), 0.0)))
    return jnp.concatenate(cos_parts, axis=0), jnp.concatenate(sin_parts, axis=0)


def _mla_prep_kernel(cq_ref, ckv_ref, kr_ref, pos_ref, qnw_ref, kvnw_ref, wq_ref, wkv_ref, rot_ref,
                     q_ref, k_ref, v_ref):
    cos_t, sin_t = _rope_tables(pos_ref[...], rot_ref[...])

    cq = cq_ref[...]
    cqn = cq * lax.rsqrt(jnp.mean(cq * cq, axis=-1, keepdims=True) + EPS) * qnw_ref[...]
    ckv = ckv_ref[...]
    ckvn = ckv * lax.rsqrt(jnp.mean(ckv * ckv, axis=-1, keepdims=True) + EPS) * kvnw_ref[...]
    cqn = cqn.astype(BF16)
    ckvn = ckvn.astype(BF16)
    k_pe = _rope(kr_ref[...], cos_t, sin_t).astype(BF16)
    scale = MLA_QK_DIM ** -0.5 * math.log2(math.e)
    for h in range(MLA_HEADS):
        c0 = h * MLA_QK_PAD
        qh = _dot(cqn, wq_ref[:, c0:c0 + MLA_QK_PAD])
        q_ref[h, :, 0:MLA_NOPE] = (qh[:, :MLA_NOPE] * scale).astype(BF16)
        q_ref[h, :, MLA_NOPE:MLA_QK_PAD] = (_rope(qh[:, MLA_NOPE:], cos_t, sin_t) * scale).astype(BF16)
        kvh = _dot(ckvn, wkv_ref[:, c0:c0 + MLA_NOPE + MLA_VDIM])
        k_ref[h, :, 0:MLA_NOPE] = kvh[:, :MLA_NOPE].astype(BF16)
        k_ref[h, :, MLA_NOPE:MLA_QK_PAD] = k_pe
        v_ref[h] = kvh[:, MLA_NOPE:].astype(BF16)


def _mla_prep(cq, ckv, kr, positions, q_norm_w, w_uq, kv_norm_w, w_ukv, batch, seq, tm):
    t = cq.shape[0]
    half = MLA_ROPE // 2
    wq = w_uq.reshape(MLA_Q_RANK, MLA_HEADS, MLA_QK_DIM)
    zq = jnp.zeros((MLA_Q_RANK, MLA_HEADS, half), w_uq.dtype)
    wq = jnp.concatenate([wq[:, :, :MLA_NOPE], wq[:, :, MLA_NOPE:MLA_NOPE + half], zq,
                          wq[:, :, MLA_NOPE + half:], zq], axis=2)
    wq = wq.reshape(MLA_Q_RANK, MLA_HEADS * MLA_QK_PAD).astype(BF16)
    wkv = w_ukv.astype(BF16)
    inv_freq = ROPE_THETA ** (-jnp.arange(half, dtype=F32) / half)
    rot = jnp.tile(inv_freq, LANES // half).reshape(1, LANES)
    pos = positions.astype(F32).reshape(t, 1)
    nblk = seq // tm
    row = lambda width: pl.BlockSpec((tm, width), lambda i: (i, 0))
    full = lambda a: pl.BlockSpec(a.shape, lambda i: (0,) * a.ndim)
    hspec = lambda width: pl.BlockSpec((None, MLA_HEADS, tm, width), lambda i: (i // nblk, 0, i % nblk, 0))
    qnw = q_norm_w.reshape(1, -1)
    kvnw = kv_norm_w.reshape(1, -1)
    return pl.pallas_call(
        _mla_prep_kernel,
        grid=(t // tm,),
        in_specs=[row(MLA_Q_RANK), row(MLA_KV_RANK), row(LANES), row(1), full(qnw), full(kvnw),
                  full(wq), full(wkv), full(rot)],
        out_specs=[hspec(MLA_QK_PAD), hspec(MLA_QK_PAD), hspec(MLA_VDIM)],
        out_shape=[jax.ShapeDtypeStruct((batch, MLA_HEADS, seq, MLA_QK_PAD), BF16),
                   jax.ShapeDtypeStruct((batch, MLA_HEADS, seq, MLA_QK_PAD), BF16),
                   jax.ShapeDtypeStruct((batch, MLA_HEADS, seq, MLA_VDIM), BF16)],
        compiler_params=_cparams(("parallel",)),
        name="mla_prep",
    )(cq, ckv, kr, pos, qnw, kvnw, wq, wkv, rot)


def _flash_kernel(q_ref, k_ref, v_ref, o_ref, v1_scr, *, tq):
    heads, seq = q_ref.shape[0], q_ref.shape[1]
    row = lax.broadcasted_iota(I32, (tq, tq), 0)
    col = lax.broadcasted_iota(I32, (tq, tq), 1)
    causal = row >= col
    items = [(h, i) for h in range(heads) for i in range(seq // tq)]

    def scores(h, i):
        q = q_ref[h, i * tq:(i + 1) * tq, :]
        return _dot_nt(q, k_ref[h, 0:(i + 1) * tq, :])

    def softmax(h, i, s):
        lo = i * tq
        sd = jnp.where(causal, s[:, lo:lo + tq], -jnp.inf)
        m = jnp.max(sd, axis=-1, keepdims=True)
        if i > 0:
            m = jnp.maximum(m, jnp.max(s[:, 0:lo], axis=-1, keepdims=True))
        p = jnp.exp2(jnp.concatenate([s[:, 0:lo], sd], axis=1) - m) if i > 0 else jnp.exp2(sd - m)
        return p.astype(BF16)

    v1_scr[:, :, 0:MLA_VDIM] = v_ref[...]
    v1_scr[:, :, MLA_VDIM:] = jnp.ones((heads, seq, MLA_VDIM), BF16)

    def output(h, i, p):
        hi = (i + 1) * tq
        acc = _dot(p, v1_scr[h, 0:hi, :])
        o_ref[i * tq:hi, h * MLA_VDIM:(h + 1) * MLA_VDIM] = (acc[:, 0:MLA_VDIM] / acc[:, MLA_VDIM:]).astype(BF16)

    s = {n: scores(*items[n]) for n in range(min(2, len(items)))}
    pending = softmax(*items[0], s.pop(0))
    for n in range(len(items)):
        if n + 2 < len(items):
            s[n + 2] = scores(*items[n + 2])
        output(*items[n], pending)
        if n + 1 < len(items):
            pending = softmax(*items[n + 1], s.pop(n + 1))


def _flash(q, k, v, tq, heads_per_step):
    batch, heads, seq, _ = q.shape
    hp = heads_per_step
    qk = pl.BlockSpec((None, hp, seq, MLA_QK_PAD), lambda b, h: (b, h, 0, 0))
    return pl.pallas_call(
        functools.partial(_flash_kernel, tq=tq),
        grid=(batch, heads // hp),
        in_specs=[qk, qk, pl.BlockSpec((None, hp, seq, MLA_VDIM), lambda b, h: (b, h, 0, 0))],
        out_specs=pl.BlockSpec((None, seq, hp * MLA_VDIM), lambda b, h: (b, 0, h)),
        out_shape=jax.ShapeDtypeStruct((batch, seq, heads * MLA_VDIM), BF16),
        scratch_shapes=[pltpu.VMEM((hp, seq, 2 * MLA_VDIM), BF16)],
        compiler_params=_cparams(("parallel", "parallel")),
        name="flash",
    )(q, k, v)


_R_E0, _R_E1, _R_W0, _R_W1, _R_RANK0, _R_RANK1 = 0, 1, 2, 3, 4, 5


def _first_lane_equal(x, value, lane):
    return jnp.min(jnp.where(x == value, lane, LANES), axis=-1, keepdims=True)


def _combine_kernel(oa_ref, ob_ref, sa_ref, sb_ref, x_ref, wa_ref, wb_ref, wo_ref, fnw_ref, wr_ref, br_ref,
                    lt_ref, x1_ref, h2_ref, rec_ref, cnt_ref, run_ref, y_scr):
    i = pl.program_id(0)

    @pl.when(i == 0)
    def _():
        run_ref[...] = jnp.zeros_like(run_ref)

    sub = lt_ref.shape[0]
    n_sub = x_ref.shape[0] // sub
    half = x_ref.shape[1] // 2
    rows = lambda s: slice(s * sub, (s + 1) * sub)

    def branches(s):
        for c in (0, half):
            cols = slice(c, c + half)
            ya = _dot(oa_ref[rows(s), :], wa_ref[:, cols])
            yb = _dot(ob_ref[rows(s), :], wb_ref[:, cols])
            y = sa_ref[rows(s), cols].astype(F32) * ya + sb_ref[rows(s), cols].astype(F32) * yb
            y_scr[rows(s), cols] = y.astype(BF16)

    def residual(s):
        for c in (0, half):
            cols = slice(c, c + half)
            x1_ref[rows(s), cols] = x_ref[rows(s), cols] + _dot(y_scr[rows(s), :], wo_ref[:, cols])

    def router(s):
        x1 = x1_ref[rows(s), :]
        h2 = x1 * lax.rsqrt(jnp.mean(x1 * x1, axis=-1, keepdims=True) + EPS) * fnw_ref[...]
        h2_ref[rows(s), :] = h2
        hh, hl = _split_bf16(h2)
        both = _dot(hh, wr_ref[...])
        return both[:, 0:LANES] + both[:, LANES:] + _dot(hl, wr_ref[:, 0:LANES]) + br_ref[...]

    run = run_ref[...]
    logits = {}
    for step in range(n_sub + 3):
        if step < n_sub:
            branches(step)
        if 0 <= step - 1 < n_sub:
            residual(step - 1)
        if 0 <= step - 2 < n_sub:
            logits[step - 2] = router(step - 2)
        if 0 <= step - 3 < n_sub:
            rec, run = _route(logits.pop(step - 3), run, lt_ref[...])
            rec_ref[rows(step - 3), :] = rec
    run_ref[...] = run
    cnt_ref[...] = run


def _route(logits, run, lt):
    tm = logits.shape[0]
    lane = lax.broadcasted_iota(I32, (tm, LANES), 1)
    neg = -jnp.inf

    gl = jnp.where(lane < N_GROUPS, logits, neg)
    ge = jnp.exp(gl - jnp.max(gl, axis=-1, keepdims=True))
    gp = ge / jnp.sum(ge, axis=-1, keepdims=True)
    g_w = jnp.max(gp, axis=-1, keepdims=True)
    g_idx = _first_lane_equal(gp, g_w, lane)

    ej = lane - EXPERT_LANE0
    sel = (ej >= 0) & (ej < N_EXPERTS) & ((ej >> 3) == g_idx)
    el = jnp.where(sel, logits, neg)
    ee = jnp.exp(el - jnp.max(el, axis=-1, keepdims=True))
    ep = jnp.where(sel, ee / jnp.sum(ee, axis=-1, keepdims=True), -1.0)
    p0 = jnp.max(ep, axis=-1, keepdims=True)
    l0 = _first_lane_equal(ep, p0, lane)
    ep1 = jnp.where(lane == l0, -1.0, ep)
    p1 = jnp.max(ep1, axis=-1, keepdims=True)
    l1 = _first_lane_equal(ep1, p1, lane)
    psum = p0 + p1
    w0 = g_w * (p0 / psum)
    w1 = g_w * (p1 / psum)

    oh0 = lane == l0
    oh1 = lane == l1
    oh = jnp.where(oh0 | oh1, 1.0, 0.0)
    before = run + _dot(lt, oh.astype(BF16))
    rank0 = jnp.sum(jnp.where(oh0, before, 0.0), axis=-1, keepdims=True)
    rank1 = jnp.sum(jnp.where(oh1, before, 0.0), axis=-1, keepdims=True)
    run = run + jnp.sum(oh, axis=0, keepdims=True)

    rec = jnp.where(lane == _R_E0, (l0 - EXPERT_LANE0).astype(F32), 0.0)
    rec = jnp.where(lane == _R_E1, (l1 - EXPERT_LANE0).astype(F32), rec)
    rec = jnp.where(lane == _R_W0, w0, rec)
    rec = jnp.where(lane == _R_W1, w1, rec)
    rec = jnp.where(lane == _R_RANK0, rank0, rec)
    rec = jnp.where(lane == _R_RANK1, rank1, rec)
    return rec, run


def _combine(o_a, o_b, sa, sb, x2, w_a, w_b, w_o, ffn_norm_w, w_group, b_group, w_expert, b_expert, tm):
    t, d = x2.shape
    pad = LANES - N_GROUPS - N_EXPERTS
    wr = jnp.concatenate([w_group, w_expert, jnp.zeros((d, pad), F32)], axis=1)
    wr_hi = wr.astype(BF16)
    wr_lo = (wr - wr_hi.astype(F32)).astype(BF16)
    wr2 = jnp.concatenate([wr_hi, wr_lo], axis=1)
    br = jnp.concatenate([b_group, b_expert, jnp.zeros((pad,), F32)]).reshape(1, LANES)
    lt = jnp.asarray(np.tril(np.ones((COMBINE_SUB, COMBINE_SUB), np.float32), -1), dtype=BF16)
    fnw = ffn_norm_w.reshape(1, d)
    row = lambda width: pl.BlockSpec((tm, width), lambda i: (i, 0))
    full = lambda a: pl.BlockSpec(a.shape, lambda i: (0,) * a.ndim)
    wa, wb, wo = w_a.astype(BF16), w_b.astype(BF16), w_o.astype(BF16)
    return pl.pallas_call(
        _combine_kernel,
        grid=(t // tm,),
        in_specs=[row(d), row(d), row(d), row(d), row(d), full(wa), full(wb), full(wo), full(fnw),
                  full(wr2), full(br), full(lt)],
        out_specs=[row(d), row(d), row(LANES), pl.BlockSpec((1, LANES), lambda i: (0, 0))],
        out_shape=[jax.ShapeDtypeStruct((t, d), F32),
                   jax.ShapeDtypeStruct((t, d), F32),
                   jax.ShapeDtypeStruct((t, LANES), F32),
                   jax.ShapeDtypeStruct((1, LANES), F32)],
        scratch_shapes=[pltpu.VMEM((1, LANES), F32), pltpu.VMEM((tm, d), BF16)],
        compiler_params=_cparams(("arbitrary",)),
        name="combine",
    )(o_a, o_b, sa, sb, x2, wa, wb, wo, fnw, wr2, br, lt)


def _row_copy(src_ref, src_row, dst_ref, dst_row, sem):
    return pltpu.make_async_copy(src_ref.at[pl.ds(src_row, 1)], dst_ref.at[pl.ds(dst_row, 1)], sem)


def _dest_kernel(rec_ref, ps_ref, d_ref):
    rec = rec_ref[...]
    lane = lax.broadcasted_iota(I32, rec.shape, 1)
    ps = ps_ref[...]

    def lookup(e_lane, r_lane):
        e = rec[:, e_lane:e_lane + 1].astype(I32) + EXPERT_LANE0
        start = jnp.sum(jnp.where(lane == e, ps, 0.0), axis=-1, keepdims=True)
        return (start + rec[:, r_lane:r_lane + 1]).astype(I32)

    d0 = lookup(_R_E0, _R_RANK0)
    d1 = lookup(_R_E1, _R_RANK1)
    d_ref[...] = jnp.where(lane == 0, d0, jnp.where(lane == 1, d1, 0))


def _dest(rec, pstart, tm):
    t = rec.shape[0]
    ps = jnp.zeros((1, LANES), F32).at[0, EXPERT_LANE0:EXPERT_LANE0 + N_EXPERTS].set(pstart.astype(F32))
    return pl.pallas_call(
        _dest_kernel,
        grid=(t // tm,),
        in_specs=[pl.BlockSpec((tm, LANES), lambda i: (i, 0)), pl.BlockSpec((1, LANES), lambda i: (0, 0))],
        out_specs=pl.BlockSpec((tm, LANES), lambda i: (i, 0)),
        out_shape=jax.ShapeDtypeStruct((t, LANES), I32),
        compiler_params=_cparams(("parallel",)),
        name="dest",
    )(rec, ps)


def _dispatch_kernel(ps_ref, cnt_ref, nu_ref, d_ref, h_ref, xs_ref, zero_scr, sem, zsem):
    tm = h_ref.shape[0]
    bm = zero_scr.shape[0]

    def zero_fill(wait):
        def expert(e, carry):
            first = ps_ref[e] + cnt_ref[e]
            end = first + ((bm - (cnt_ref[e] & (bm - 1))) & (bm - 1))
            tiles = jnp.minimum((first + SUBLANES - 1) & -SUBLANES, end)

            def row(r, c):
                copy = _row_copy(zero_scr, 0, xs_ref, r, zsem)
                copy.wait() if wait else copy.start()
                return c

            def tile(j, c):
                r = pl.multiple_of(tiles + j * SUBLANES, SUBLANES)
                copy = pltpu.make_async_copy(zero_scr.at[pl.ds(0, SUBLANES)], xs_ref.at[pl.ds(r, SUBLANES)], zsem)
                copy.wait() if wait else copy.start()
                return c

            lax.fori_loop(first, tiles, row, 0)
            lax.fori_loop(0, (end - tiles) >> (SUBLANES.bit_length() - 1), tile, 0)
            return carry

        lax.fori_loop(0, cnt_ref.shape[0], expert, 0)

        def tail(blk, carry):
            copy = pltpu.make_async_copy(zero_scr, xs_ref.at[pl.ds(blk * bm, bm)], zsem)
            copy.wait() if wait else copy.start()
            return carry

        lax.fori_loop(nu_ref[0], xs_ref.shape[0] // bm, tail, 0)

    @pl.when(pl.program_id(0) == 0)
    def _():
        zero_scr[...] = jnp.zeros_like(zero_scr)
        zero_fill(wait=False)

    def start(j, carry):
        for u in range(DMA_UNROLL):
            tok = j * DMA_UNROLL + u
            for k in range(2):
                _row_copy(h_ref, tok, xs_ref, d_ref[2 * tok + k], sem).start(priority=k)
        return carry

    lax.fori_loop(0, tm // DMA_UNROLL, start, 0)

    def wait(j, carry):
        for _ in range(2 * DMA_UNROLL):
            _row_copy(h_ref, 0, xs_ref, 0, sem).wait()
        return carry

    lax.fori_loop(0, tm // DMA_UNROLL, wait, 0)

    @pl.when(pl.program_id(0) == pl.num_programs(0) - 1)
    def _():
        zero_fill(wait=True)


def _dispatch(h2, dest, pstart, counts, n_used, cap, tm):
    t, d = h2.shape
    return pl.pallas_call(
        _dispatch_kernel,
        grid_spec=pltpu.PrefetchScalarGridSpec(
            num_scalar_prefetch=3, grid=(t // tm,),
            in_specs=[pl.BlockSpec((2 * tm,), lambda i, ps, cnt, nu: (i,), memory_space=pltpu.SMEM),
                      pl.BlockSpec((tm, d), lambda i, ps, cnt, nu: (i, 0))],
            out_specs=pl.BlockSpec(memory_space=pl.ANY),
            scratch_shapes=[pltpu.VMEM((MOE_ROWS, d), h2.dtype), pltpu.SemaphoreType.DMA(()),
                            pltpu.SemaphoreType.DMA(())]),
        out_shape=jax.ShapeDtypeStruct((cap, d), h2.dtype),
        compiler_params=_cparams(("arbitrary",)),
        name="dispatch",
    )(pstart, counts, n_used, dest, h2)


def _experts_kernel(be_ref, nu_ref, seg_ref, nxt_ref, xs_ref, w1_hbm, w3_hbm, w2_hbm, ys_ref,
                    w1f, w3f, w2f, w1b, w3b, w2b, sems):
    i = pl.program_id(0)

    def weight_copies(e, slot):
        return [pltpu.make_async_copy(hbm.at[e], buf.at[slot], sems.at[slot, j])
                for j, (hbm, buf) in enumerate(((w1_hbm, w1f), (w3_hbm, w3f), (w2_hbm, w2f)))]

    @pl.when(i < nu_ref[0])
    def _():
        changed = jnp.logical_or(i == 0, be_ref[i] != be_ref[jnp.maximum(i - 1, 0)])
        slot = seg_ref[i] & 1

        @pl.when(i == 0)
        def _():
            for copy in weight_copies(be_ref[0], 0):
                copy.start()

        @pl.when(changed)
        def _():
            for copy in weight_copies(be_ref[i], slot):
                copy.wait()
            w1b[...] = w1f[slot].astype(BF16)
            w3b[...] = w3f[slot].astype(BF16)
            w2b[...] = w2f[slot].astype(BF16)

            @pl.when(nxt_ref[i] >= 0)
            def _():
                for copy in weight_copies(nxt_ref[i], 1 - slot):
                    copy.start()

        grp = xs_ref.shape[0] // MOE_ROW_GROUPS
        rows = lambda g: slice(g * grp, (g + 1) * grp)

        def up(g):
            x = xs_ref[rows(g), :].astype(BF16)
            return _dot(x, w1b[...]), _dot(x, w3b[...])

        def down(g, h1, h3):
            hid = h1 * jax.nn.sigmoid(h1) * h3
            ys_ref[rows(g), :] = _dot(hid.astype(BF16), w2b[...])

        nxt = up(0)
        for g in range(MOE_ROW_GROUPS):
            cur = nxt
            if g + 1 < MOE_ROW_GROUPS:
                nxt = up(g + 1)
            down(g, *cur)

    @pl.when(i >= nu_ref[0])
    def _():
        ys_ref[...] = jnp.zeros_like(ys_ref)


def _experts(xs, w1, w3, w2, block_expert, n_used, segment, next_expert):
    cap, d = xs.shape
    bm = MOE_ROWS
    ff = w1.shape[2]
    hbm = pl.BlockSpec(memory_space=pl.ANY)
    return pl.pallas_call(
        _experts_kernel,
        grid_spec=pltpu.PrefetchScalarGridSpec(
            num_scalar_prefetch=4, grid=(cap // bm,),
            in_specs=[pl.BlockSpec((bm, d), lambda i, be, nu, seg, nxt: (jnp.minimum(i, nu[0] - 1), 0)),
                      hbm, hbm, hbm],
            out_specs=pl.BlockSpec((bm, d), lambda i, be, nu, seg, nxt: (i, 0)),
            scratch_shapes=[pltpu.VMEM((2, d, ff), F32), pltpu.VMEM((2, d, ff), F32), pltpu.VMEM((2, ff, d), F32),
                            pltpu.VMEM((d, ff), BF16), pltpu.VMEM((d, ff), BF16), pltpu.VMEM((ff, d), BF16),
                            pltpu.SemaphoreType.DMA((2, 3))]),
        out_shape=jax.ShapeDtypeStruct((cap, d), F32),
        compiler_params=_cparams(("arbitrary",)),
        name="experts",
    )(block_expert, n_used, segment, next_expert, xs, w1, w3, w2)


def _final_kernel(d_ref, dn_ref, x1_ref, rec_ref, nw_ref, ys_ref, o_ref, buf, sems):
    i = pl.program_id(0)
    tm = x1_ref.shape[0]
    slot = i & 1

    def gather(idx_ref, s):
        def start(j, carry):
            for u in range(DMA_UNROLL):
                tok = j * DMA_UNROLL + u
                for k in range(2):
                    _row_copy(ys_ref, idx_ref[2 * tok + k], buf.at[s, k], tok, sems.at[s]).start(priority=k)
            return carry

        lax.fori_loop(0, tm // DMA_UNROLL, start, 0)

    @pl.when(i == 0)
    def _():
        gather(d_ref, 0)

    @pl.when(i + 1 < pl.num_programs(0))
    def _():
        gather(dn_ref, 1 - slot)

    def wait(j, carry):
        for _ in range(2 * DMA_UNROLL):
            _row_copy(ys_ref, 0, buf.at[slot, 0], 0, sems.at[slot]).wait()
        return carry

    lax.fori_loop(0, tm // DMA_UNROLL, wait, 0)

    rec = rec_ref[...]
    x = x1_ref[...] + buf[slot, 0] * rec[:, _R_W0:_R_W0 + 1] + buf[slot, 1] * rec[:, _R_W1:_R_W1 + 1]
    o_ref[...] = x * lax.rsqrt(jnp.mean(x * x, axis=-1, keepdims=True) + EPS) * nw_ref[...]


def _final(x1, rec, ys, dest, final_norm_w, tm):
    t, d = x1.shape
    nw = final_norm_w.reshape(1, d)
    last = t // tm - 1
    return pl.pallas_call(
        _final_kernel,
        grid=(t // tm,),
        in_specs=[pl.BlockSpec((2 * tm,), lambda i: (i,), memory_space=pltpu.SMEM),
                  pl.BlockSpec((2 * tm,), lambda i: (jnp.minimum(i + 1, last),), memory_space=pltpu.SMEM),
                  pl.BlockSpec((tm, d), lambda i: (i, 0)),
                  pl.BlockSpec((tm, LANES), lambda i: (i, 0)),
                  pl.BlockSpec((1, d), lambda i: (0, 0)),
                  pl.BlockSpec(memory_space=pl.ANY)],
        out_specs=pl.BlockSpec((tm, d), lambda i: (i, 0)),
        out_shape=jax.ShapeDtypeStruct((t, d), F32),
        scratch_shapes=[pltpu.VMEM((2, 2, tm, d), F32), pltpu.SemaphoreType.DMA((2,))],
        compiler_params=_cparams(("arbitrary",)),
        name="final",
    )(dest, dest, x1, rec, nw, ys)


def _moe_layout(counts):
    bm = MOE_ROWS
    padded = (counts + bm - 1) // bm * bm
    pends = jnp.cumsum(padded)
    pstart = pends - padded
    return pstart.astype(I32), pends.astype(I32)


def kernel(x, positions, attn_norm_w, w_in, hg_lower_bound, hg_out_norm_w, mla_q_norm_w, mla_w_uq, mla_kv_norm_w, mla_w_ukv, w_branch_hgrn, w_branch_mla, w_out, ffn_norm_w, router_group_w, router_group_b, router_expert_w, router_expert_b, expert_w1, expert_w3, expert_w2, final_norm_w):
    batch, seq, d = x.shape
    t = batch * seq
    depth = w_in.shape[0]
    assert d == 1024 and seq % 512 == 0 and t % 1024 == 0
    x2 = x.reshape(t, d)
    for l in range(depth):
        q, kk, lf, v, g, cq, ckv, kr, sa, sb = _in_proj(x2, attn_norm_w[l], w_in[l], hg_lower_bound, l, tm=256)
        o_a = _hgrn(q, kk, lf, v, g, hg_out_norm_w[l], batch, seq)
        mq, mk, mv = _mla_prep(cq, ckv, kr, positions, mla_q_norm_w[l], mla_w_uq[l], mla_kv_norm_w[l],
                               mla_w_ukv[l], batch, seq, tm=512)
        o_b = _flash(mq, mk, mv, tq=256, heads_per_step=4).reshape(t, MLA_HEADS * MLA_VDIM)
        x1, h2, rec, cnt = _combine(o_a, o_b, sa, sb, x2, w_branch_hgrn[l], w_branch_mla[l], w_out[l],
                                    ffn_norm_w[l], router_group_w[l], router_group_b[l],
                                    router_expert_w[l], router_expert_b[l], tm=1024)

        counts = cnt[0, EXPERT_LANE0:EXPERT_LANE0 + N_EXPERTS].astype(I32)
        pstart, pends = _moe_layout(counts)
        n_blocks = (2 * t) // MOE_ROWS + N_EXPERTS
        cap = n_blocks * MOE_ROWS
        n_used = (pends[-1] // MOE_ROWS).reshape(1)
        blk = jnp.minimum(jnp.arange(n_blocks, dtype=I32), n_used[0] - 1)
        be = jnp.sum((blk[:, None] * MOE_ROWS >= pends[None, :]).astype(I32), axis=1)
        eid = jnp.arange(N_EXPERTS, dtype=I32)
        nonempty = counts > 0
        ordinal = jnp.cumsum(nonempty.astype(I32)) - 1
        later = nonempty[None, :] & (eid[None, :] > eid[:, None])
        nxt_e = jnp.min(jnp.where(later, eid[None, :], N_EXPERTS), axis=1)
        nxt_e = jnp.where(nxt_e == N_EXPERTS, -1, nxt_e)
        onehot = (be[:, None] == eid[None, :]).astype(I32)
        segment = jnp.sum(onehot * ordinal[None, :], axis=1)
        next_expert = jnp.sum(onehot * nxt_e[None, :], axis=1)
        dest = _dest(rec, pstart, tm=math.gcd(t, 2048))[:, 0:2].reshape(2 * t)

        xs = _dispatch(h2, dest, pstart, counts, n_used, cap, tm=512)
        ys = _experts(xs, expert_w1[l], expert_w3[l], expert_w2[l], be, n_used, segment, next_expert)
        assert l == depth - 1, "multi-layer stacking needs an un-normalised residual output"
        out = _final(x1, rec, ys, dest, final_norm_w, tm=512)
    return out.reshape(batch, seq, d)
```
